```python
import math
import jax, jax.numpy as jnp
from jax import lax
import numpy as np

D_MODEL = 4096
BATCH = 8
SEQ = 2048
DEPTH = 4

CHUNK = 64
Q_BLOCK = 128
MLA_HEADS = 16
Q_LORA = 768
KV_LORA = 512
QK_NOPE = 128
QK_ROPE = 64
V_DIM = 128
ROPE_THETA = 10000.0
MLA_SCALE = (QK_NOPE + QK_ROPE) ** -0.5
CONV_DIM = 1024
CONV_WIDTH = 31
MEM_LEN = 256
MEM_HEADS = 4
MEM_HEAD_DIM = 256
MEM_SCALE = MEM_HEAD_DIM ** -0.5
N_BRANCH = 3
N_EXPERTS = 16
N_GROUPS = 4
EXPERTS_PER_GROUP = N_EXPERTS // N_GROUPS
TOP_K = 2
D_EXPERT = 512
MOE_BLOCK = 128
DN_ALPHA = (2 * DEPTH) ** 0.25
DN_BETA = (8 * DEPTH) ** -0.25
LN_EPS = 1e-5
RMS_EPS = 1e-6
IN_SIZES = (Q_LORA, KV_LORA, QK_ROPE, 2 * CONV_DIM, MEM_HEADS * MEM_HEAD_DIM, N_BRANCH * D_MODEL)
D_IN = Q_LORA + KV_LORA + QK_ROPE + 2 * CONV_DIM + MEM_HEADS * MEM_HEAD_DIM + N_BRANCH * D_MODEL

kernel_name = 'hybrid_mla_conformer_memory_grouped_moe_deepnorm'


def split_last(t, sizes):
    out, o = [], 0
    for s in sizes:
        out.append(t[..., o:o + s])
        o += s
    return out


def layer_norm(x, g, b):
    xf = x.astype(jnp.float32)
    mu = jnp.mean(xf, -1, keepdims=True)
    var = jnp.mean(jnp.square(xf - mu), -1, keepdims=True)
    return ((xf - mu) * lax.rsqrt(var + LN_EPS) * g + b).astype(x.dtype)


def rms_norm(x, g):
    xf = x.astype(jnp.float32)
    return (xf * lax.rsqrt(jnp.mean(jnp.square(xf), -1, keepdims=True) + RMS_EPS) * g).astype(x.dtype)


def rope_tables(positions):
    inv = 1.0 / (ROPE_THETA ** (jnp.arange(0, QK_ROPE, 2, dtype=jnp.float32) / QK_ROPE))
    ang = positions.astype(jnp.float32)[..., None] * inv
    return jnp.cos(ang), jnp.sin(ang)


def apply_rope(t, cos, sin):
    tf = t.astype(jnp.float32)
    t1, t2 = tf[..., :QK_ROPE // 2], tf[..., QK_ROPE // 2:]
    return jnp.concatenate([t1 * cos - t2 * sin, t1 * sin + t2 * cos], -1).astype(t.dtype)


def chunk_causal_attention(q, k, v):
    B, S, H, Dk = q.shape
    nq = S // Q_BLOCK
    key_chunk = jnp.arange(S) // CHUNK
    qb = q.reshape(B, nq, Q_BLOCK, H, Dk).transpose(1, 0, 2, 3, 4)

    def block(args):
        q_blk, i = args
        s = jnp.einsum('bqhd,bkhd->bhqk', q_blk, k, preferred_element_type=jnp.float32) * MLA_SCALE
        q_chunk = (i * Q_BLOCK + jnp.arange(Q_BLOCK)) // CHUNK
        mask = key_chunk[None, :] <= q_chunk[:, None]
        p = jax.nn.softmax(jnp.where(mask, s, -jnp.inf), axis=-1).astype(v.dtype)
        return jnp.einsum('bhqk,bkhd->bqhd', p, v)

    o = lax.map(block, (qb, jnp.arange(nq)))
    return o.transpose(1, 0, 2, 3, 4).reshape(B, S, H, v.shape[-1])


def causal_depthwise_conv(u, w, b):
    out = lax.conv_general_dilated(
        u, w[:, None, :].astype(u.dtype), window_strides=(1,),
        padding=[(CONV_WIDTH - 1, 0)], dimension_numbers=('NWC', 'WIO', 'NWC'),
        feature_group_count=u.shape[-1])
    return out + b


def mixer(x, cos, sin, mem, w_in, q_norm_g, kv_norm_g, w_uq, w_ukv, w_o_mla, conv_w, conv_b,
          conv_ln_g, conv_ln_b, w_conv_out, w_mem_kv, w_o_mem, w_out):
    B, S, D = x.shape
    h = x @ w_in
    c_q, c_kv, k_r, conv_in, m_q, gates = split_last(h, IN_SIZES)
    q = (rms_norm(c_q, q_norm_g) @ w_uq).reshape(B, S, MLA_HEADS, QK_NOPE + QK_ROPE)
    q_rope = apply_rope(q[..., QK_NOPE:], cos[:, :, None, :], sin[:, :, None, :])
    q = jnp.concatenate([q[..., :QK_NOPE], q_rope], -1)
    kv = (rms_norm(c_kv, kv_norm_g) @ w_ukv).reshape(B, S, MLA_HEADS, QK_NOPE + V_DIM)
    k_r = apply_rope(k_r, cos, sin)
    k = jnp.concatenate([kv[..., :QK_NOPE],
                         jnp.broadcast_to(k_r[:, :, None, :], (B, S, MLA_HEADS, QK_ROPE))], -1)
    v = kv[..., QK_NOPE:]
    o = chunk_causal_attention(q, k, v).reshape(B, S, MLA_HEADS * V_DIM)
    y_mla = o @ w_o_mla
    a, gl = split_last(conv_in, (CONV_DIM, CONV_DIM))
    u = causal_depthwise_conv(a * jax.nn.sigmoid(gl), conv_w, conv_b)
    u = jax.nn.silu(layer_norm(u, conv_ln_g, conv_ln_b))
    y_conv = u @ w_conv_out
    M = mem.shape[1]
    kv_m = (mem @ w_mem_kv).reshape(B, M, 2, MEM_HEADS, MEM_HEAD_DIM)
    qm = m_q.reshape(B, S, MEM_HEADS, MEM_HEAD_DIM)
    sm = jnp.einsum('bshd,bmhd->bhsm', qm, kv_m[:, :, 0], preferred_element_type=jnp.float32) * MEM_SCALE
    pm = jax.nn.softmax(sm, axis=-1).astype(x.dtype)
    om = jnp.einsum('bhsm,bmhd->bshd', pm, kv_m[:, :, 1]).reshape(B, S, MEM_HEADS * MEM_HEAD_DIM)
    y_mem = om @ w_o_mem
    g_mla, g_conv, g_mem = split_last(jax.nn.sigmoid(gates), (D, D, D))
    return (g_mla * y_mla + g_conv * y_conv + g_mem * y_mem) @ w_out


def route(xt, w_router, b_router):
    T = xt.shape[0]
    scores = jax.nn.sigmoid(jnp.dot(xt, w_router, preferred_element_type=jnp.float32))
    bg = (scores + b_router.astype(jnp.float32)).reshape(T, N_GROUPS, EXPERTS_PER_GROUP)
    group_score = jnp.sum(lax.top_k(bg, 2)[0], -1)
    g = jnp.argmax(group_score, -1).astype(jnp.int32)
    in_group = jnp.take_along_axis(bg, g[:, None, None], axis=1)[:, 0]
    _, local = lax.top_k(in_group, TOP_K)
    idx = (g[:, None] * EXPERTS_PER_GROUP + local).astype(jnp.int32)
    sel = jnp.take_along_axis(scores, idx, -1)
    return idx, sel / jnp.sum(sel, -1, keepdims=True)


def moe_ffn(xt, idx, wts, w_gate, w_up, w_down):
    T, D = xt.shape
    A = T * TOP_K
    flat_e = idx.reshape(A)
    flat_tok = jnp.arange(A, dtype=jnp.int32) // TOP_K
    flat_w = wts.reshape(A)
    order = jnp.argsort(flat_e)
    sorted_e = flat_e[order]
    counts = jnp.bincount(flat_e, length=N_EXPERTS)
    padded = (counts + MOE_BLOCK - 1) // MOE_BLOCK * MOE_BLOCK
    seg_end = jnp.cumsum(padded)
    pad_start = seg_end - padded
    start = jnp.cumsum(counts) - counts
    dest = pad_start[sorted_e] + jnp.arange(A, dtype=jnp.int32) - start[sorted_e]
    n_blocks = (A + MOE_BLOCK - 1) // MOE_BLOCK + N_EXPERTS
    P = n_blocks * MOE_BLOCK
    slot_tok = jnp.full((P,), T, jnp.int32).at[dest].set(flat_tok[order])
    slot_w = jnp.zeros((P,), jnp.float32).at[dest].set(flat_w[order])
    block_e = jnp.minimum(
        jnp.searchsorted(seg_end, jnp.arange(n_blocks, dtype=jnp.int32) * MOE_BLOCK, side='right'),
        N_EXPERTS - 1).astype(jnp.int32)
    x_pad = jnp.concatenate([xt, jnp.zeros((1, D), xt.dtype)], 0)

    def step(acc, blk):
        tok, w, e = blk
        xb = x_pad[tok]
        hb = jax.nn.silu(xb @ w_gate[e]) * (xb @ w_up[e])
        yb = (hb @ w_down[e]) * w[:, None].astype(xt.dtype)
        return acc.at[tok].add(yb), None

    acc, _ = lax.scan(step, jnp.zeros((T + 1, D), xt.dtype),
                      (slot_tok.reshape(n_blocks, MOE_BLOCK), slot_w.reshape(n_blocks, MOE_BLOCK), block_e))
    return acc[:T]


def setup_inputs(seed: int = 0) -> dict:
    key = jax.random.key(seed)
    ks = jax.random.split(key, 26)
    f32 = jnp.float32
    L = DEPTH

    def nrm(k, shape, scale):
        return jax.random.normal(k, shape, f32) * scale

    x = jax.random.normal(ks[0], (BATCH, SEQ, D_MODEL), f32)
    mem = jax.random.normal(ks[1], (BATCH, MEM_LEN, D_MODEL), f32)
    offsets = jax.random.randint(ks[2], (BATCH, 1), 0, 512) * CHUNK
    positions = (offsets + jnp.arange(SEQ)[None, :]).astype(jnp.int32)
    return {
        'x': x,
        'mem': mem,
        'positions': positions,
        'w_in': nrm(ks[3], (L, D_MODEL, D_IN), D_MODEL ** -0.5),
        'q_norm_g': 1.0 + nrm(ks[4], (L, Q_LORA), 0.02),
        'kv_norm_g': 1.0 + nrm(ks[5], (L, KV_LORA), 0.02),
        'w_uq': nrm(ks[6], (L, Q_LORA, MLA_HEADS * (QK_NOPE + QK_ROPE)), Q_LORA ** -0.5),
        'w_ukv': nrm(ks[7], (L, KV_LORA, MLA_HEADS * (QK_NOPE + V_DIM)), KV_LORA ** -0.5),
        'w_o_mla': nrm(ks[8], (L, MLA_HEADS * V_DIM, D_MODEL), (MLA_HEADS * V_DIM) ** -0.5),
        'conv_w': nrm(ks[9], (L, CONV_WIDTH, CONV_DIM), CONV_WIDTH ** -0.5),
        'conv_b': nrm(ks[10], (L, CONV_DIM), 0.02),
        'conv_ln_g': 1.0 + nrm(ks[11], (L, CONV_DIM), 0.02),
        'conv_ln_b': nrm(ks[12], (L, CONV_DIM), 0.02),
        'w_conv_out': nrm(ks[13], (L, CONV_DIM, D_MODEL), CONV_DIM ** -0.5),
        'w_mem_kv': nrm(ks[14], (L, D_MODEL, 2 * MEM_HEADS * MEM_HEAD_DIM), D_MODEL ** -0.5),
        'w_o_mem': nrm(ks[15], (L, MEM_HEADS * MEM_HEAD_DIM, D_MODEL), (MEM_HEADS * MEM_HEAD_DIM) ** -0.5),
        'w_out': nrm(ks[16], (L, D_MODEL, D_MODEL), DN_BETA * D_MODEL ** -0.5),
        'ln1_g': 1.0 + nrm(ks[17], (L, D_MODEL), 0.02),
        'ln1_b': nrm(ks[18], (L, D_MODEL), 0.02),
        'w_router': nrm(ks[19], (D_MODEL, N_EXPERTS), D_MODEL ** -0.5),
        'b_router': nrm(ks[20], (N_EXPERTS,), 0.01),
        'w_gate': nrm(ks[21], (L, N_EXPERTS, D_MODEL, D_EXPERT), D_MODEL ** -0.5),
        'w_up': nrm(ks[22], (L, N_EXPERTS, D_MODEL, D_EXPERT), D_MODEL ** -0.5),
        'w_down': nrm(ks[23], (L, N_EXPERTS, D_EXPERT, D_MODEL), DN_BETA * D_EXPERT ** -0.5),
        'ln2_g': 1.0 + nrm(ks[24], (L, D_MODEL), 0.02),
        'ln2_b': nrm(ks[25], (L, D_MODEL), 0.02),
    }


def reference(x, mem, positions, w_in, q_norm_g, kv_norm_g, w_uq, w_ukv, w_o_mla, conv_w, conv_b,
              conv_ln_g, conv_ln_b, w_conv_out, w_mem_kv, w_o_mem, w_out, ln1_g, ln1_b,
              w_router, b_router, w_gate, w_up, w_down, ln2_g, ln2_b):
    B, S, D = x.shape
    cos, sin = rope_tables(positions)
    for l in range(DEPTH):
        y = mixer(x, cos, sin, mem, w_in[l], q_norm_g[l], kv_norm_g[l], w_uq[l], w_ukv[l], w_o_mla[l],
                  conv_w[l], conv_b[l], conv_ln_g[l], conv_ln_b[l], w_conv_out[l], w_mem_kv[l],
                  w_o_mem[l], w_out[l])
        x = layer_norm(DN_ALPHA * x + y, ln1_g[l], ln1_b[l])
        xt = x.reshape(B * S, D)
        idx, wts = route(xt, w_router, b_router)
        y = moe_ffn(xt, idx, wts, w_gate[l], w_up[l], w_down[l]).reshape(B, S, D)
        x = layer_norm(DN_ALPHA * x + y, ln2_g[l], ln2_b[l])
    return x
```

```python
import functools

import jax
import jax.numpy as jnp
from jax import lax
from jax.experimental import pallas as pl
from jax.experimental.pallas import tpu as pltpu

F32 = jnp.float32
BF16 = jnp.bfloat16

CHUNK = 64
MLA_HEADS = 16
QK_NOPE = 128
QK_ROPE = 64
V_DIM = 128
ROPE_THETA = 10000.0
MLA_SCALE = (QK_NOPE + QK_ROPE) ** -0.5
CONV_WIDTH = 31
MEM_HEADS = 4
MEM_HEAD_DIM = 256
MEM_SCALE = MEM_HEAD_DIM ** -0.5
N_EXPERTS = 16
N_GROUPS = 4
EXPERTS_PER_GROUP = N_EXPERTS // N_GROUPS
TOP_K = 2
LN_EPS = 1e-5
RMS_EPS = 1e-6

LANES = 128
Q_HEAD_PAD = 2 * LANES
CONV_HALO = 32
VMEM_LIMIT = 56 * 1024 * 1024
NEG_BIG = -1e30


def _params(dims):
    return pltpu.CompilerParams(dimension_semantics=dims, vmem_limit_bytes=VMEM_LIMIT)


def _blk(dim, pref):
    b = min(dim, pref)
    assert dim % b == 0, (dim, pref)
    return b


def _resident(shape, index_map):
    return pl.BlockSpec(shape, index_map, pipeline_mode=pl.Buffered(1))


def _dot(a, b):
    return jnp.dot(a, b, preferred_element_type=F32)


def _dot_nt(a, b):
    return lax.dot_general(a, b, (((1,), (1,)), ((), ())), preferred_element_type=F32)


def _layer_norm_rows(v, g, b):
    mu = jnp.mean(v, -1, keepdims=True)
    d = v - mu
    var = jnp.mean(d * d, -1, keepdims=True)
    return d * lax.rsqrt(var + LN_EPS) * g + b


def _rope_lanes(t, c, s1, s2):
    return t * c + pltpu.roll(t, 32, 1) * s1 + pltpu.roll(t, 96, 1) * s2


def _mla_proj_kernel(x_ref, wa_ref, wuq_ref, wukv_ref, qg_ref, kvg_ref, c_ref, s1_ref, s2_ref,
                     q_ref, kv_ref, kr_ref, *, q_lora, kv_lora):
    h = _dot(x_ref[...], wa_ref[...])
    cq = h[:, :q_lora]
    ckv = h[:, q_lora:q_lora + kv_lora]
    kr = h[:, q_lora + kv_lora:]

    def rms(t, g):
        return t * lax.rsqrt(jnp.mean(t * t, -1, keepdims=True) + RMS_EPS) * g

    c, s1, s2 = c_ref[...], s1_ref[...], s2_ref[...]
    kr_ref[...] = _rope_lanes(kr, c, s1, s2).astype(BF16)
    kv_ref[...] = _dot(rms(ckv, kvg_ref[...]).astype(BF16), wukv_ref[...]).astype(BF16)
    q = _dot(rms(cq, qg_ref[...]).astype(BF16), wuq_ref[...])
    for hh in range(MLA_HEADS):
        lo = hh * Q_HEAD_PAD
        q_ref[:, lo:lo + LANES] = q[:, lo:lo + LANES].astype(BF16)
        q_ref[:, lo + LANES:lo + Q_HEAD_PAD] = _rope_lanes(
            q[:, lo + LANES:lo + Q_HEAD_PAD], c, s1, s2).astype(BF16)


def _mla_proj(x_bf, wa, wuq, wukv, qg, kvg, c_tab, s1_tab, s2_tab):
    t, d = x_bf.shape
    q_lora, kv_lora = qg.shape[1], kvg.shape[1]
    na = wa.shape[1]
    nq, nkv = wuq.shape[1], wukv.shape[1]
    bm = _blk(t, 256)
    row = lambda i: (i, 0)
    fixed = lambda i: (0, 0)
    return pl.pallas_call(
        functools.partial(_mla_proj_kernel, q_lora=q_lora, kv_lora=kv_lora),
        grid=(t // bm,),
        in_specs=[
            pl.BlockSpec((bm, d), row),
            _resident((d, na), fixed),
            _resident((q_lora, nq), fixed),
            _resident((kv_lora, nkv), fixed),
            _resident((1, q_lora), fixed),
            _resident((1, kv_lora), fixed),
            pl.BlockSpec((bm, LANES), row),
            pl.BlockSpec((bm, LANES), row),
            pl.BlockSpec((bm, LANES), row),
        ],
        out_specs=[
            pl.BlockSpec((bm, nq), row),
            pl.BlockSpec((bm, nkv), row),
            pl.BlockSpec((bm, LANES), row),
        ],
        out_shape=[
            jax.ShapeDtypeStruct((t, nq), BF16),
            jax.ShapeDtypeStruct((t, nkv), BF16),
            jax.ShapeDtypeStruct((t, LANES), BF16),
        ],
        compiler_params=_params(("parallel",)),
        name="mla_proj",
    )(x_bf, wa, wuq, wukv, qg, kvg, c_tab, s1_tab, s2_tab)


def _attn_kernel(q_ref, kv_ref, kr_ref, o_ref, kcat_ref, *, seq, tq):
    kcat_ref[:, :LANES] = kv_ref[0, :, :LANES]
    kcat_ref[:, LANES:] = kr_ref[0]
    rq = lax.broadcasted_iota(jnp.int32, (tq, tq), 0) // CHUNK
    ck = lax.broadcasted_iota(jnp.int32, (tq, tq), 1) // CHUNK
    diag_ok = ck <= rq
    for i in range(seq // tq):
        lo = i * tq
        q = q_ref[0, lo:lo + tq, :]
        s_d = jnp.where(diag_ok, _dot_nt(q, kcat_ref[lo:lo + tq, :]) * MLA_SCALE, NEG_BIG)
        m = jnp.max(s_d, -1, keepdims=True)
        if i > 0:
            s_f = _dot_nt(q, kcat_ref[:lo, :]) * MLA_SCALE
            m = jnp.maximum(m, jnp.max(s_f, -1, keepdims=True))
        p_d = jnp.exp(s_d - m)
        l = jnp.sum(p_d, -1, keepdims=True)
        o = _dot(p_d.astype(BF16), kv_ref[0, lo:lo + tq, LANES:])
        if i > 0:
            p_f = jnp.exp(s_f - m)
            l = l + jnp.sum(p_f, -1, keepdims=True)
            o = o + _dot(p_f.astype(BF16), kv_ref[0, :lo, LANES:])
        o_ref[0, lo:lo + tq, :] = (o / l).astype(BF16)


def _attention(q, kv, kr, batch, seq):
    tq = _blk(seq, 256)
    return pl.pallas_call(
        functools.partial(_attn_kernel, seq=seq, tq=tq),
        grid=(batch, MLA_HEADS),
        in_specs=[
            pl.BlockSpec((1, seq, Q_HEAD_PAD), lambda b, h: (b, 0, h)),
            pl.BlockSpec((1, seq, QK_NOPE + V_DIM), lambda b, h: (b, 0, h)),
            pl.BlockSpec((1, seq, LANES), lambda b, h: (b, 0, 0)),
        ],
        out_specs=pl.BlockSpec((1, seq, V_DIM), lambda b, h: (b, 0, h)),
        out_shape=jax.ShapeDtypeStruct((batch, seq, MLA_HEADS * V_DIM), BF16),
        scratch_shapes=[pltpu.VMEM((seq, Q_HEAD_PAD), BF16)],
        compiler_params=_params(("parallel", "parallel")),
        name="mla_attention",
    )(q, kv, kr)


def _xproj_kernel(x_ref, wa_ref, wg_ref, wm_ref, u_ref, mq_ref):
    x = x_ref[...]
    u_ref[...] = _dot(x, wa_ref[...]) * jax.nn.sigmoid(_dot(x, wg_ref[...]))
    mq_ref[...] = _dot(x, wm_ref[...]).astype(BF16)


def _xproj(x_bf, w_glu_a, w_glu_g, w_mq):
    t, d = x_bf.shape
    n = w_glu_a.shape[1]
    assert w_mq.shape[1] == n
    bm, bn = _blk(t, 512), _blk(n, 512)
    wspec = pl.BlockSpec((d, bn), lambda j, i: (0, j))
    ospec = pl.BlockSpec((bm, bn), lambda j, i: (i, j))
    return pl.pallas_call(
        _xproj_kernel,
        grid=(n // bn, t // bm),
        in_specs=[pl.BlockSpec((bm, d), lambda j, i: (i, 0)), wspec, wspec, wspec],
        out_specs=[ospec, ospec],
        out_shape=[jax.ShapeDtypeStruct((t, n), F32), jax.ShapeDtypeStruct((t, n), BF16)],
        compiler_params=_params(("parallel", "parallel")),
        name="xproj_glu_memq",
    )(x_bf, w_glu_a, w_glu_g, w_mq)


def _mm_kernel(a_ref, w_ref, o_ref):
    o_ref[...] = _dot(a_ref[...], w_ref[...]).astype(o_ref.dtype)


def _matmul(a, w, out_dtype):
    m, k = a.shape
    n = w.shape[1]
    bm, bn = _blk(m, 512), _blk(n, 512)
    return pl.pallas_call(
        _mm_kernel,
        grid=(n // bn, m // bm),
        in_specs=[pl.BlockSpec((bm, k), lambda j, i: (i, 0)), pl.BlockSpec((k, bn), lambda j, i: (0, j))],
        out_specs=pl.BlockSpec((bm, bn), lambda j, i: (i, j)),
        out_shape=jax.ShapeDtypeStruct((m, n), out_dtype),
        compiler_params=_params(("parallel", "parallel")),
        name="matmul",
    )(a, w)


def _conv_kernel(cur_ref, prev_ref, w_ref, b_ref, g_ref, beta_ref, o_ref, win_ref, *, ts, rt):
    i = pl.program_id(1)
    win_ref[0:CONV_HALO, :] = jnp.where(i > 0, prev_ref[0, ts - CONV_HALO:ts, :], 0.0)
    win_ref[CONV_HALO:CONV_HALO + ts, :] = cur_ref[0]
    first_tap = CONV_HALO - (CONV_WIDTH - 1)
    for r in range(ts // rt):
        r0 = r * rt
        acc = win_ref[r0 + first_tap:r0 + first_tap + rt, :] * w_ref[0:1, :]
        for k in range(1, CONV_WIDTH):
            acc = acc + win_ref[r0 + first_tap + k:r0 + first_tap + k + rt, :] * w_ref[k:k + 1, :]
        v = _layer_norm_rows(acc + b_ref[...], g_ref[...], beta_ref[...])
        o_ref[0, r0:r0 + rt, :] = (v * jax.nn.sigmoid(v)).astype(BF16)


def _conv_module(u0, conv_w, conv_b, ln_g, ln_b, batch, seq):
    c = u0.shape[-1]
    ts = _blk(seq, 256)
    assert ts >= CONV_HALO
    rt = _blk(ts, 16)
    vec = lambda b, i: (0, 0)
    return pl.pallas_call(
        functools.partial(_conv_kernel, ts=ts, rt=rt),
        grid=(batch, seq // ts),
        in_specs=[
            pl.BlockSpec((1, ts, c), lambda b, i: (b, i, 0)),
            pl.BlockSpec((1, ts, c), lambda b, i: (b, jnp.maximum(i - 1, 0), 0)),
            pl.BlockSpec((CONV_HALO, c), vec),
            pl.BlockSpec((1, c), vec),
            pl.BlockSpec((1, c), vec),
            pl.BlockSpec((1, c), vec),
        ],
        out_specs=pl.BlockSpec((1, ts, c), lambda b, i: (b, i, 0)),
        out_shape=jax.ShapeDtypeStruct((batch, seq, c), BF16),
        scratch_shapes=[pltpu.VMEM((CONV_HALO + ts, c), F32)],
        compiler_params=_params(("parallel", "arbitrary")),
        name="conv_module",
    )(u0, u0, conv_w, conv_b, ln_g, ln_b)


def _mem_attn_kernel(q_ref, kv_ref, o_ref):
    width = MEM_HEADS * MEM_HEAD_DIM
    for h in range(MEM_HEADS):
        lo = h * MEM_HEAD_DIM
        s = _dot_nt(q_ref[0, :, lo:lo + MEM_HEAD_DIM], kv_ref[0, :, lo:lo + MEM_HEAD_DIM]) * MEM_SCALE
        p = jnp.exp(s - jnp.max(s, -1, keepdims=True))
        l = jnp.sum(p, -1, keepdims=True)
        o = _dot(p.astype(BF16), kv_ref[0, :, width + lo:width + lo + MEM_HEAD_DIM])
        o_ref[0, :, lo:lo + MEM_HEAD_DIM] = (o / l).astype(BF16)


def _mem_attention(mq, kvm, batch, seq):
    width = MEM_HEADS * MEM_HEAD_DIM
    m = kvm.shape[1]
    ts = _blk(seq, 512)
    return pl.pallas_call(
        _mem_attn_kernel,
        grid=(batch, seq // ts),
        in_specs=[
            pl.BlockSpec((1, ts, width), lambda b, i: (b, i, 0)),
            pl.BlockSpec((1, m, 2 * width), lambda b, i: (b, 0, 0)),
        ],
        out_specs=pl.BlockSpec((1, ts, width), lambda b, i: (b, i, 0)),
        out_shape=jax.ShapeDtypeStruct((batch, seq, width), BF16),
        compiler_params=_params(("parallel", "parallel")),
        name="mem_attention",
    )(mq, kvm)


def _merge_kernel(x_ref, o_ref, u_ref, m_ref, wg0_ref, wg1_ref, wg2_ref, w0_ref, w1_ref, w2_ref, z_ref):
    x = x_ref[...]
    z = jax.nn.sigmoid(_dot(x, wg0_ref[...])) * _dot(o_ref[...], w0_ref[...])
    z = z + jax.nn.sigmoid(_dot(x, wg1_ref[...])) * _dot(u_ref[...], w1_ref[...])
    z = z + jax.nn.sigmoid(_dot(x, wg2_ref[...])) * _dot(m_ref[...], w2_ref[...])
    z_ref[...] = z.astype(BF16)


def _merge(x_bf, o_mla, u_conv, o_mem, w_gates, w_o_mla, w_conv_out, w_o_mem):
    t, d = x_bf.shape
    bm, bn = _blk(t, 512), _blk(d, 256)
    nj = d // bn
    act = lambda a: pl.BlockSpec((bm, a.shape[1]), lambda j, i: (i, 0))
    gate = lambda b: pl.BlockSpec((d, bn), lambda j, i, b=b: (0, j + b * nj))
    wbr = lambda w: pl.BlockSpec((w.shape[0], bn), lambda j, i: (0, j))
    return pl.pallas_call(
        _merge_kernel,
        grid=(nj, t // bm),
        in_specs=[act(x_bf), act(o_mla), act(u_conv), act(o_mem), gate(0), gate(1), gate(2),
                  wbr(w_o_mla), wbr(w_conv_out), wbr(w_o_mem)],
        out_specs=pl.BlockSpec((bm, bn), lambda j, i: (i, j)),
        out_shape=jax.ShapeDtypeStruct((t, d), BF16),
        compiler_params=_params(("parallel", "parallel")),
        name="gated_merge",
    )(x_bf, o_mla, u_conv, o_mem, w_gates, w_gates, w_gates, w_o_mla, w_conv_out, w_o_mem)


def _route_columns(logits, b_router):
    scores = jax.nn.sigmoid(logits)
    biased = scores + b_router
    sc = [scores[:, e:e + 1] for e in range(N_EXPERTS)]
    bg = [biased[:, e:e + 1] for e in range(N_EXPERTS)]
    group_score = []
    for g in range(N_GROUPS):
        v = bg[g * EXPERTS_PER_GROUP:(g + 1) * EXPERTS_PER_GROUP]
        best = None
        for a in range(EXPERTS_PER_GROUP):
            for b in range(a + 1, EXPERTS_PER_GROUP):
                pair = v[a] + v[b]
                best = pair if best is None else jnp.maximum(best, pair)
        group_score.append(best)
    best, best_g = group_score[0], jnp.zeros_like(group_score[0], dtype=jnp.int32)
    for g in range(1, N_GROUPS):
        better = group_score[g] > best
        best = jnp.where(better, group_score[g], best)
        best_g = jnp.where(better, g, best_g)
    sel, picked = [], []
    for e in range(N_EXPERTS):
        g, j = divmod(e, EXPERTS_PER_GROUP)
        rank = jnp.zeros_like(best_g)
        for jj in range(EXPERTS_PER_GROUP):
            if jj == j:
                continue
            other = bg[g * EXPERTS_PER_GROUP + jj]
            ahead = (other > bg[e]) | ((other == bg[e]) & (jj < j))
            rank = rank + ahead.astype(jnp.int32)
        s = (best_g == g) & (rank < TOP_K)
        sel.append(s)
        picked.append(jnp.where(s, sc[e], 0.0))
    denom = picked[0]
    for e in range(1, N_EXPERTS):
        denom = denom + picked[e]
    return [p / denom for p in picked], sel


def _outproj_ln_kernel(z_ref, w_ref, x_ref, g_ref, b_ref, wr_ref, br_ref,
                       xo_ref, xbf_ref, route_ref, acc_ref, *, alpha, nj, bn):
    j = pl.program_id(1)
    acc_ref[j] = alpha * x_ref[...] + _dot(z_ref[...], w_ref[...])

    @pl.when(j == nj - 1)
    def _():
        d = nj * bn
        tot = jnp.sum(acc_ref[0], -1, keepdims=True)
        for jj in range(1, nj):
            tot = tot + jnp.sum(acc_ref[jj], -1, keepdims=True)
        mu = tot / d
        sq = None
        for jj in range(nj):
            dv = acc_ref[jj] - mu
            part = jnp.sum(dv * dv, -1, keepdims=True)
            sq = part if sq is None else sq + part
        rstd = lax.rsqrt(sq / d + LN_EPS)
        logits = None
        for jj in range(nj):
            lo = jj * bn
            v = (acc_ref[jj] - mu) * rstd * g_ref[:, lo:lo + bn] + b_ref[:, lo:lo + bn]
            xo_ref[:, lo:lo + bn] = v
            vb = v.astype(BF16)
            xbf_ref[:, lo:lo + bn] = vb
            part = _dot(vb, wr_ref[lo:lo + bn, :])
            logits = part if logits is None else logits + part
        cw, sel = _route_columns(logits, br_ref[...])
        lane = lax.broadcasted_iota(jnp.int32, route_ref.shape, 1)
        packed = jnp.zeros(route_ref.shape, F32)
        for e in range(N_EXPERTS):
            packed = jnp.where(lane == e, cw[e], packed)
            packed = jnp.where(lane == N_EXPERTS + e, sel[e].astype(F32), packed)
        route_ref[...] = packed


def _outproj_ln_route(z, w_out, x_res, ln_g, ln_b, w_router, b_router, alpha):
    t, d = z.shape
    bm, bn = _blk(t, 256), _blk(d, 512)
    nj = d // bn
    fixed = lambda i, j: (0, 0)
    return pl.pallas_call(
        functools.partial(_outproj_ln_kernel, alpha=alpha, nj=nj, bn=bn),
        grid=(t // bm, nj),
        in_specs=[
            pl.BlockSpec((bm, d), lambda i, j: (i, 0)),
            pl.BlockSpec((d, bn), lambda i, j: (0, j)),
            pl.BlockSpec((bm, bn), lambda i, j: (i, j)),
            _resident((1, d), fixed),
            _resident((1, d), fixed),
            _resident((d, LANES), fixed),
            _resident((1, LANES), fixed),
        ],
        out_specs=[
            pl.BlockSpec((bm, d), lambda i, j: (i, 0)),
            pl.BlockSpec((bm, d), lambda i, j: (i, 0)),
            pl.BlockSpec((bm, LANES), lambda i, j: (i, 0)),
        ],
        out_shape=[
            jax.ShapeDtypeStruct((t, d), F32),
            jax.ShapeDtypeStruct((t, d), BF16),
            jax.ShapeDtypeStruct((t, LANES), F32),
        ],
        scratch_shapes=[pltpu.VMEM((nj, bm, bn), F32)],
        compiler_params=_params(("parallel", "arbitrary")),
        name="outproj_ln_route",
    )(z, w_out, x_res, ln_g, ln_b, w_router, b_router)


def _row_copy(src_hbm, row, dst_ref, r, sem):
    return pltpu.make_async_copy(src_hbm.at[pl.ds(row, 1), :], dst_ref.at[pl.ds(r, 1), :], sem)


def _gather_kernel(nused_ref, tok_ref, x_hbm, o_ref, sem, *, rows):
    live = pl.program_id(0) < nused_ref[0]

    @pl.when(jnp.logical_not(live))
    def _():
        o_ref[...] = jnp.zeros(o_ref.shape, o_ref.dtype)

    @pl.when(live)
    def _():
        def issue(r, carry):
            _row_copy(x_hbm, tok_ref[0, 0, r], o_ref, r, sem).start()
            return carry

        lax.fori_loop(0, rows, issue, 0)

        def drain(r, carry):
            _row_copy(x_hbm, 0, o_ref, r, sem).wait()
            return carry

        lax.fori_loop(0, rows, drain, 0)


def _gather_rows(x, slot_tok, n_used, rows):
    p = slot_tok.shape[0]
    d = x.shape[1]
    nb = p // rows
    live = lambda i, nu: jnp.minimum(i, nu[0] - 1)
    return pl.pallas_call(
        functools.partial(_gather_kernel, rows=rows),
        grid_spec=pltpu.PrefetchScalarGridSpec(
            num_scalar_prefetch=1,
            grid=(nb,),
            in_specs=[
                pl.BlockSpec((1, 1, rows), lambda i, nu: (live(i, nu), 0, 0), memory_space=pltpu.SMEM),
                pl.BlockSpec(memory_space=pl.ANY),
            ],
            out_specs=pl.BlockSpec((rows, d), lambda i, nu: (i, 0)),
            scratch_shapes=[pltpu.SemaphoreType.DMA(())],
        ),
        out_shape=jax.ShapeDtypeStruct((p, d), x.dtype),
        compiler_params=_params(("arbitrary",)),
        name="moe_gather",
    )(n_used, slot_tok.reshape(nb, 1, rows), x)


def _ffn_kernel(be_ref, nused_ref, x_ref, wg_ref, wu_ref, wd_ref, y_ref):
    live = pl.program_id(0) < nused_ref[0]

    @pl.when(jnp.logical_not(live))
    def _():
        y_ref[...] = jnp.zeros(y_ref.shape, y_ref.dtype)

    @pl.when(live)
    def _():
        xb = x_ref[...].astype(BF16)
        g = _dot(xb, wg_ref[0])
        h = (g * jax.nn.sigmoid(g)) * _dot(xb, wu_ref[0])
        y_ref[...] = _dot(h.astype(BF16), wd_ref[0])


def _expert_ffn(xs, block_e, n_used, w_gate, w_up, w_down, rows):
    p, d = xs.shape
    de = w_gate.shape[2]
    nb = p // rows
    live = lambda i, be, nu: jnp.minimum(i, nu[0] - 1)
    wsel = lambda i, be, nu: (be[live(i, be, nu)], 0, 0)
    return pl.pallas_call(
        _ffn_kernel,
        grid_spec=pltpu.PrefetchScalarGridSpec(
            num_scalar_prefetch=2,
            grid=(nb,),
            in_specs=[
                pl.BlockSpec((rows, d), lambda i, be, nu: (live(i, be, nu), 0)),
                pl.BlockSpec((1, d, de), wsel),
                pl.BlockSpec((1, d, de), wsel),
                pl.BlockSpec((1, de, d), wsel),
            ],
            out_specs=pl.BlockSpec((rows, d), lambda i, be, nu: (i, 0)),
        ),
        out_shape=jax.ShapeDtypeStruct((p, d), F32),
        compiler_params=_params(("arbitrary",)),
        name="moe_expert_ffn",
    )(block_e, n_used, xs, w_gate, w_up, w_down)


def _combine_kernel(pos_ref, ys_hbm, x_ref, wts_ref, g_ref, b_ref, xo_ref, xbf_ref, buf0, buf1, sem,
                    *, alpha, rows):
    def issue(r, carry):
        _row_copy(ys_hbm, pos_ref[0, 0, r], buf0, r, sem).start()
        _row_copy(ys_hbm, pos_ref[0, 1, r], buf1, r, sem).start()
        return carry

    lax.fori_loop(0, rows, issue, 0)

    def drain(r, carry):
        _row_copy(ys_hbm, 0, buf0, r, sem).wait()
        _row_copy(ys_hbm, 0, buf1, r, sem).wait()
        return carry

    lax.fori_loop(0, rows, drain, 0)
    y = wts_ref[:, 0:1] * buf0[...] + wts_ref[:, 1:2] * buf1[...]
    v = _layer_norm_rows(alpha * x_ref[...] + y, g_ref[...], b_ref[...])
    xo_ref[...] = v
    xbf_ref[...] = v.astype(BF16)


def _combine_ln(ys, pos, wts, x_res, ln_g, ln_b, alpha):
    t, d = x_res.shape
    rows = _blk(t, 256)
    nb = t // rows
    fixed = lambda i: (0, 0)
    return pl.pallas_call(
        functools.partial(_combine_kernel, alpha=alpha, rows=rows),
        grid=(nb,),
        in_specs=[
            pl.BlockSpec((1, 2, rows), lambda i: (i, 0, 0), memory_space=pltpu.SMEM),
            pl.BlockSpec(memory_space=pl.ANY),
            pl.BlockSpec((rows, d), lambda i: (i, 0)),
            pl.BlockSpec((rows, LANES), lambda i: (i, 0)),
            _resident((1, d), fixed),
            _resident((1, d), fixed),
        ],
        out_specs=[pl.BlockSpec((rows, d), lambda i: (i, 0)), pl.BlockSpec((rows, d), lambda i: (i, 0))],
        out_shape=[jax.ShapeDtypeStruct((t, d), F32), jax.ShapeDtypeStruct((t, d), BF16)],
        scratch_shapes=[pltpu.VMEM((rows, d), F32), pltpu.VMEM((rows, d), F32), pltpu.SemaphoreType.DMA(())],
        compiler_params=_params(("arbitrary",)),
        name="moe_combine_ln",
    )(pos.reshape(nb, rows, 2).transpose(0, 2, 1), ys, x_res, wts, ln_g, ln_b)


def _dispatch_plan(route, rows):
    t = route.shape[0]
    cw = route[:, :N_EXPERTS]
    sel = route[:, N_EXPERTS:2 * N_EXPERTS] > 0.5
    sel_i = sel.astype(jnp.int32)
    csum = jnp.cumsum(sel_i, axis=0)
    counts = csum[-1]
    padded = (counts + rows - 1) // rows * rows
    seg_end = jnp.cumsum(padded)
    dest = (seg_end - padded)[None, :] + (csum - sel_i)
    n_slots = (t * TOP_K // rows + N_EXPERTS) * rows
    pos0 = jnp.min(jnp.where(sel, dest, n_slots), axis=1)
    pos1 = jnp.max(jnp.where(sel, dest, -1), axis=1)
    w0 = jnp.sum(jnp.where(sel & (dest == pos0[:, None]), cw, 0.0), axis=1)
    w1 = jnp.sum(jnp.where(sel & (dest == pos1[:, None]), cw, 0.0), axis=1)
    tok = jnp.arange(t, dtype=jnp.int32)
    slot_tok = jnp.zeros((n_slots,), jnp.int32).at[jnp.concatenate([pos0, pos1])].set(
        jnp.concatenate([tok, tok]), unique_indices=True)
    nb = n_slots // rows
    block_e = jnp.minimum(
        jnp.searchsorted(seg_end, jnp.arange(nb, dtype=jnp.int32) * rows, side="right"),
        N_EXPERTS - 1).astype(jnp.int32)
    n_used = (seg_end[-1] // rows).astype(jnp.int32).reshape(1)
    pos = jnp.stack([pos0, pos1], axis=1).astype(jnp.int32)
    wts = jnp.zeros((t, LANES), F32).at[:, 0].set(w0).at[:, 1].set(w1)
    return slot_tok, block_e, n_used, pos, wts


def _rope_tables(positions):
    inv = 1.0 / (ROPE_THETA ** (jnp.arange(0, QK_ROPE, 2, dtype=F32) / QK_ROPE))
    ang = positions.astype(F32).reshape(-1, 1) * inv
    cos, sin = jnp.cos(ang), jnp.sin(ang)
    zero = jnp.zeros_like(cos)
    pad = jnp.zeros((cos.shape[0], LANES - QK_ROPE), F32)
    c_tab = jnp.concatenate([cos, cos, pad], 1)
    s1_tab = jnp.concatenate([zero, sin, pad], 1)
    s2_tab = jnp.concatenate([-sin, zero, pad], 1)
    return c_tab, s1_tab, s2_tab


def kernel(x, mem, positions, w_in, q_norm_g, kv_norm_g, w_uq, w_ukv, w_o_mla, conv_w, conv_b, conv_ln_g, conv_ln_b, w_conv_out, w_mem_kv, w_o_mem, w_out, ln1_g, ln1_b, w_router, b_router, w_gate, w_up, w_down, ln2_g, ln2_b):
    batch, seq, d = x.shape
    depth = w_in.shape[0]
    t = batch * seq
    q_lora, kv_lora, conv_dim = q_norm_g.shape[1], kv_norm_g.shape[1], conv_w.shape[2]
    mem_w = MEM_HEADS * MEM_HEAD_DIM
    alpha = (2 * depth) ** 0.25
    moe_rows = 256

    c_tab, s1_tab, s2_tab = _rope_tables(positions)
    mem_bf = mem.reshape(-1, d).astype(BF16)
    w_router_p = jnp.pad(w_router, ((0, 0), (0, LANES - N_EXPERTS))).astype(BF16)
    b_router_p = jnp.pad(b_router.astype(F32), (0, LANES - N_EXPERTS)).reshape(1, LANES)

    o_kr = q_lora + kv_lora
    o_conv = o_kr + QK_ROPE
    o_mq = o_conv + 2 * conv_dim
    o_gate = o_mq + mem_w

    xf = x.reshape(t, d)
    xb = xf.astype(BF16)
    for l in range(depth):
        wi = w_in[l]
        wa = jnp.pad(wi[:, :o_conv], ((0, 0), (0, LANES - QK_ROPE))).astype(BF16)
        wuq = jnp.pad(w_uq[l].reshape(q_lora, MLA_HEADS, QK_NOPE + QK_ROPE),
                      ((0, 0), (0, 0), (0, Q_HEAD_PAD - QK_NOPE - QK_ROPE))).reshape(q_lora, -1).astype(BF16)
        q, kv, kr = _mla_proj(xb, wa, wuq, w_ukv[l].astype(BF16), q_norm_g[l][None], kv_norm_g[l][None],
                              c_tab, s1_tab, s2_tab)
        o_mla = _attention(q.reshape(batch, seq, -1), kv.reshape(batch, seq, -1),
                           kr.reshape(batch, seq, LANES), batch, seq).reshape(t, -1)

        u0, mq = _xproj(xb, wi[:, o_conv:o_conv + conv_dim].astype(BF16),
                        wi[:, o_conv + conv_dim:o_mq].astype(BF16), wi[:, o_mq:o_gate].astype(BF16))
        u_conv = _conv_module(u0.reshape(batch, seq, conv_dim),
                              jnp.pad(conv_w[l], ((0, CONV_HALO - CONV_WIDTH), (0, 0))),
                              conv_b[l][None], conv_ln_g[l][None], conv_ln_b[l][None], batch, seq).reshape(t, -1)

        kvm = _matmul(mem_bf, w_mem_kv[l].astype(BF16), BF16)
        o_mem = _mem_attention(mq.reshape(batch, seq, mem_w), kvm.reshape(batch, -1, 2 * mem_w),
                               batch, seq).reshape(t, -1)

        z = _merge(xb, o_mla, u_conv, o_mem, wi[:, o_gate:].astype(BF16), w_o_mla[l].astype(BF16),
                   w_conv_out[l].astype(BF16), w_o_mem[l].astype(BF16))
        x1, _, route = _outproj_ln_route(z, w_out[l].astype(BF16), xf, ln1_g[l][None], ln1_b[l][None],
                                         w_router_p, b_router_p, alpha)

        slot_tok, block_e, n_used, pos, wts = _dispatch_plan(route, moe_rows)
        xs = _gather_rows(x1, slot_tok, n_used, moe_rows)
        ys = _expert_ffn(xs, block_e, n_used, w_gate[l].astype(BF16), w_up[l].astype(BF16),
                         w_down[l].astype(BF16), moe_rows)
        xf, xb = _combine_ln(ys, pos, wts, x1, ln2_g[l][None], ln2_b[l][None], alpha)
    return xf.reshape(batch, seq, d)
```

```python
import functools

import jax
import jax.numpy as jnp
from jax import lax
from jax.experimental import pallas as pl
from jax.experimental.pallas import tpu as pltpu

F32 = jnp.float32
BF16 = jnp.bfloat16

CHUNK = 64
MLA_HEADS = 16
QK_NOPE = 128
QK_ROPE = 64
V_DIM = 128
ROPE_THETA = 10000.0
MLA_SCALE = (QK_NOPE + QK_ROPE) ** -0.5
CONV_WIDTH = 31
MEM_HEADS = 4
MEM_HEAD_DIM = 256
MEM_SCALE = MEM_HEAD_DIM ** -0.5
N_EXPERTS = 16
N_GROUPS = 4
EXPERTS_PER_GROUP = N_EXPERTS // N_GROUPS
TOP_K = 2
LN_EPS = 1e-5
RMS_EPS = 1e-6

LANES = 128
SUBLANES = 8
Q_HEAD_PAD = 2 * LANES
CONV_HALO = 32
W_IN_ALIGN = 512
ROUTE_ROWS = 2 * N_EXPERTS
VMEM_LIMIT = 56 * 1024 * 1024
NEG_BIG = -1e30


def _params(dims):
    return pltpu.CompilerParams(dimension_semantics=dims, vmem_limit_bytes=VMEM_LIMIT)


def _blk(dim, pref):
    b = min(dim, pref)
    assert dim % b == 0, (dim, pref)
    return b


def _round_up(n, m):
    return -(-n // m) * m


def _resident(shape, index_map):
    return pl.BlockSpec(shape, index_map, pipeline_mode=pl.Buffered(1))


def _dot(a, b):
    return jnp.dot(a, b, preferred_element_type=F32)


def _dot_nt(a, b):
    return lax.dot_general(a, b, (((1,), (1,)), ((), ())), preferred_element_type=F32)


def _layer_norm_rows(v, g, b):
    mu = jnp.mean(v, -1, keepdims=True)
    d = v - mu
    var = jnp.mean(d * d, -1, keepdims=True)
    return d * lax.rsqrt(var + LN_EPS) * g + b


def _rope_lanes(t, c, s1, s2):
    return t * c + pltpu.roll(t, 32, 1) * s1 + pltpu.roll(t, 96, 1) * s2


def _mla_proj_kernel(x_ref, wa_ref, wuq_ref, wukv_ref, qg_ref, kvg_ref, c_ref, s1_ref, s2_ref,
                     q_ref, kv_ref, kr_ref, *, q_lora, kv_lora):
    h = _dot(x_ref[...], wa_ref[...])
    cq = h[:, :q_lora]
    ckv = h[:, q_lora:q_lora + kv_lora]
    kr = h[:, q_lora + kv_lora:q_lora + kv_lora + LANES]

    def rms(t, g):
        return t * lax.rsqrt(jnp.mean(t * t, -1, keepdims=True) + RMS_EPS) * g

    c, s1, s2 = c_ref[...], s1_ref[...], s2_ref[...]
    kr_ref[...] = _rope_lanes(kr, c, s1, s2).astype(BF16)
    kv_ref[...] = _dot(rms(ckv, kvg_ref[...]).astype(BF16), wukv_ref[...]).astype(BF16)
    q = _dot(rms(cq, qg_ref[...]).astype(BF16), wuq_ref[...])
    for hh in range(MLA_HEADS):
        lo = hh * Q_HEAD_PAD
        q_ref[:, lo:lo + LANES] = q[:, lo:lo + LANES].astype(BF16)
        q_ref[:, lo + LANES:lo + Q_HEAD_PAD] = _rope_lanes(
            q[:, lo + LANES:lo + Q_HEAD_PAD], c, s1, s2).astype(BF16)


def _mla_proj(l, x_bf, w_in_p, wa_cols, wuq, wukv, qg, kvg, c_tab, s1_tab, s2_tab):
    t, d = x_bf.shape
    q_lora, kv_lora = qg.shape[2], kvg.shape[2]
    nq, nkv = wuq.shape[2], wukv.shape[2]
    bm = _blk(t, 256)
    row = lambda i: (i, 0)
    layer = lambda i: (l, 0, 0)
    return pl.pallas_call(
        functools.partial(_mla_proj_kernel, q_lora=q_lora, kv_lora=kv_lora),
        grid=(t // bm,),
        in_specs=[
            pl.BlockSpec((bm, d), row),
            _resident((None, d, wa_cols), layer),
            _resident((None, q_lora, nq), layer),
            _resident((None, kv_lora, nkv), layer),
            _resident((None, 1, q_lora), layer),
            _resident((None, 1, kv_lora), layer),
            pl.BlockSpec((bm, LANES), row),
            pl.BlockSpec((bm, LANES), row),
            pl.BlockSpec((bm, LANES), row),
        ],
        out_specs=[
            pl.BlockSpec((bm, nq), row),
            pl.BlockSpec((bm, nkv), row),
            pl.BlockSpec((bm, LANES), row),
        ],
        out_shape=[
            jax.ShapeDtypeStruct((t, nq), BF16),
            jax.ShapeDtypeStruct((t, nkv), BF16),
            jax.ShapeDtypeStruct((t, LANES), BF16),
        ],
        compiler_params=_params(("parallel",)),
        name="mla_proj",
    )(x_bf, w_in_p, wuq, wukv, qg, kvg, c_tab, s1_tab, s2_tab)


def _attn_kernel(q_ref, kv_ref, kr_ref, o_ref, kcat_ref, *, seq, tq):
    kcat_ref[:, :LANES] = kv_ref[0, :, :LANES]
    kcat_ref[:, LANES:] = kr_ref[0]
    rq = lax.broadcasted_iota(jnp.int32, (tq, tq), 0) // CHUNK
    ck = lax.broadcasted_iota(jnp.int32, (tq, tq), 1) // CHUNK
    diag_ok = ck <= rq
    for i in range(seq // tq):
        lo = i * tq
        q = q_ref[0, lo:lo + tq, :]
        s_d = jnp.where(diag_ok, _dot_nt(q, kcat_ref[lo:lo + tq, :]) * MLA_SCALE, NEG_BIG)
        m = jnp.max(s_d, -1, keepdims=True)
        if i > 0:
            s_f = _dot_nt(q, kcat_ref[:lo, :]) * MLA_SCALE
            m = jnp.maximum(m, jnp.max(s_f, -1, keepdims=True))
        p_d = jnp.exp(s_d - m)
        l = jnp.sum(p_d, -1, keepdims=True)
        o = _dot(p_d.astype(BF16), kv_ref[0, lo:lo + tq, LANES:])
        if i > 0:
            p_f = jnp.exp(s_f - m)
            l = l + jnp.sum(p_f, -1, keepdims=True)
            o = o + _dot(p_f.astype(BF16), kv_ref[0, :lo, LANES:])
        o_ref[0, lo:lo + tq, :] = (o / l).astype(BF16)


def _attention(q, kv, kr, batch, seq):
    tq = _blk(seq, 256)
    return pl.pallas_call(
        functools.partial(_attn_kernel, seq=seq, tq=tq),
        grid=(batch, MLA_HEADS),
        in_specs=[
            pl.BlockSpec((1, seq, Q_HEAD_PAD), lambda b, h: (b, 0, h)),
            pl.BlockSpec((1, seq, QK_NOPE + V_DIM), lambda b, h: (b, 0, h)),
            pl.BlockSpec((1, seq, LANES), lambda b, h: (b, 0, 0)),
        ],
        out_specs=pl.BlockSpec((1, seq, V_DIM), lambda b, h: (b, 0, h)),
        out_shape=jax.ShapeDtypeStruct((batch, seq, MLA_HEADS * V_DIM), BF16),
        scratch_shapes=[pltpu.VMEM((seq, Q_HEAD_PAD), BF16)],
        compiler_params=_params(("parallel", "parallel")),
        name="mla_attention",
    )(q, kv, kr)


def _xproj_kernel(x_ref, wa_ref, wg_ref, wm_ref, u_ref, mq_ref):
    x = x_ref[...]
    u_ref[...] = _dot(x, wa_ref[...]) * jax.nn.sigmoid(_dot(x, wg_ref[...]))
    mq_ref[...] = _dot(x, wm_ref[...]).astype(BF16)


def _xproj(l, x_bf, w_in_p, col_a, col_g, col_m, n):
    t, d = x_bf.shape
    bm, bn = _blk(t, 512), _blk(n, 512)
    wspec = lambda col: pl.BlockSpec((None, d, bn), lambda j, i: (l, 0, col // bn + j))
    ospec = pl.BlockSpec((bm, bn), lambda j, i: (i, j))
    return pl.pallas_call(
        _xproj_kernel,
        grid=(n // bn, t // bm),
        in_specs=[pl.BlockSpec((bm, d), lambda j, i: (i, 0)), wspec(col_a), wspec(col_g), wspec(col_m)],
        out_specs=[ospec, ospec],
        out_shape=[jax.ShapeDtypeStruct((t, n), F32), jax.ShapeDtypeStruct((t, n), BF16)],
        compiler_params=_params(("parallel", "parallel")),
        name="xproj_glu_memq",
    )(x_bf, w_in_p, w_in_p, w_in_p)


def _mm_kernel(a_ref, w_ref, o_ref):
    o_ref[...] = _dot(a_ref[...], w_ref[...]).astype(o_ref.dtype)


def _matmul(l, a, w, out_dtype):
    m, k = a.shape
    n = w.shape[2]
    bm, bn = _blk(m, 512), _blk(n, 512)
    return pl.pallas_call(
        _mm_kernel,
        grid=(n // bn, m // bm),
        in_specs=[pl.BlockSpec((bm, k), lambda j, i: (i, 0)), pl.BlockSpec((None, k, bn), lambda j, i: (l, 0, j))],
        out_specs=pl.BlockSpec((bm, bn), lambda j, i: (i, j)),
        out_shape=jax.ShapeDtypeStruct((m, n), out_dtype),
        compiler_params=_params(("parallel", "parallel")),
        name="matmul",
    )(a, w)


def _conv_kernel(cur_ref, prev_ref, w_ref, b_ref, g_ref, beta_ref, o_ref, win_ref, *, ts, rt):
    i = pl.program_id(1)
    win_ref[0:CONV_HALO, :] = jnp.where(i > 0, prev_ref[0, ts - CONV_HALO:ts, :], 0.0)
    win_ref[CONV_HALO:CONV_HALO + ts, :] = cur_ref[0]
    win_ref[CONV_HALO + ts:, :] = jnp.zeros((SUBLANES, win_ref.shape[1]), F32)
    first_tap = CONV_HALO - (CONV_WIDTH - 1)
    for r in range(ts // rt):
        r0 = r * rt
        acc = None
        for phase in range(SUBLANES):
            ys = None
            for k in range(CONV_WIDTH):
                off = first_tap + k
                if off % SUBLANES != phase:
                    continue
                base = r0 + off - phase
                term = win_ref[base:base + rt + SUBLANES, :] * w_ref[k:k + 1, :]
                ys = term if ys is None else ys + term
            part = ys[phase:phase + rt, :]
            acc = part if acc is None else acc + part
        v = _layer_norm_rows(acc + b_ref[...], g_ref[...], beta_ref[...])
        o_ref[0, r0:r0 + rt, :] = (v * jax.nn.sigmoid(v)).astype(BF16)


def _conv_module(l, u0, conv_w, conv_b, ln_g, ln_b, batch, seq):
    c = u0.shape[-1]
    ts = _blk(seq, 256)
    assert ts >= CONV_HALO
    rt = _blk(ts, 32)
    vec = lambda b, i: (l, 0, 0)
    return pl.pallas_call(
        functools.partial(_conv_kernel, ts=ts, rt=rt),
        grid=(batch, seq // ts),
        in_specs=[
            pl.BlockSpec((1, ts, c), lambda b, i: (b, i, 0)),
            pl.BlockSpec((1, ts, c), lambda b, i: (b, jnp.maximum(i - 1, 0), 0)),
            pl.BlockSpec((None, CONV_HALO, c), vec),
            pl.BlockSpec((None, 1, c), vec),
            pl.BlockSpec((None, 1, c), vec),
            pl.BlockSpec((None, 1, c), vec),
        ],
        out_specs=pl.BlockSpec((1, ts, c), lambda b, i: (b, i, 0)),
        out_shape=jax.ShapeDtypeStruct((batch, seq, c), BF16),
        scratch_shapes=[pltpu.VMEM((CONV_HALO + ts + SUBLANES, c), F32)],
        compiler_params=_params(("parallel", "arbitrary")),
        name="conv_module",
    )(u0, u0, conv_w, conv_b, ln_g, ln_b)


def _mem_attn_kernel(q_ref, kv_ref, o_ref):
    width = MEM_HEADS * MEM_HEAD_DIM
    for h in range(MEM_HEADS):
        lo = h * MEM_HEAD_DIM
        s = _dot_nt(q_ref[0, :, lo:lo + MEM_HEAD_DIM], kv_ref[0, :, lo:lo + MEM_HEAD_DIM]) * MEM_SCALE
        p = jnp.exp(s - jnp.max(s, -1, keepdims=True))
        l = jnp.sum(p, -1, keepdims=True)
        o = _dot(p.astype(BF16), kv_ref[0, :, width + lo:width + lo + MEM_HEAD_DIM])
        o_ref[0, :, lo:lo + MEM_HEAD_DIM] = (o / l).astype(BF16)


def _mem_attention(mq, kvm, batch, seq):
    width = MEM_HEADS * MEM_HEAD_DIM
    m = kvm.shape[1]
    ts = _blk(seq, 512)
    return pl.pallas_call(
        _mem_attn_kernel,
        grid=(batch, seq // ts),
        in_specs=[
            pl.BlockSpec((1, ts, width), lambda b, i: (b, i, 0)),
            pl.BlockSpec((1, m, 2 * width), lambda b, i: (b, 0, 0)),
        ],
        out_specs=pl.BlockSpec((1, ts, width), lambda b, i: (b, i, 0)),
        out_shape=jax.ShapeDtypeStruct((batch, seq, width), BF16),
        compiler_params=_params(("parallel", "parallel")),
        name="mem_attention",
    )(mq, kvm)


def _merge_kernel(x_ref, o_ref, u_ref, m_ref, wg0_ref, wg1_ref, wg2_ref, w0_ref, w1_ref, w2_ref, z_ref):
    x = x_ref[...]
    z = jax.nn.sigmoid(_dot(x, wg0_ref[...])) * _dot(o_ref[...], w0_ref[...])
    z = z + jax.nn.sigmoid(_dot(x, wg1_ref[...])) * _dot(u_ref[...], w1_ref[...])
    z = z + jax.nn.sigmoid(_dot(x, wg2_ref[...])) * _dot(m_ref[...], w2_ref[...])
    z_ref[...] = z.astype(BF16)


def _merge(l, x_bf, o_mla, u_conv, o_mem, w_in_p, col_gate, w_o_mla, w_conv_out, w_o_mem):
    t, d = x_bf.shape
    bm, bn = _blk(t, 512), _blk(d, 256)
    nj = d // bn
    act = lambda a: pl.BlockSpec((bm, a.shape[1]), lambda j, i: (i, 0))
    gate = lambda b: pl.BlockSpec((None, d, bn), lambda j, i, b=b: (l, 0, col_gate // bn + b * nj + j))
    wbr = lambda w: pl.BlockSpec((None, w.shape[1], bn), lambda j, i: (l, 0, j))
    return pl.pallas_call(
        _merge_kernel,
        grid=(nj, t // bm),
        in_specs=[act(x_bf), act(o_mla), act(u_conv), act(o_mem), gate(0), gate(1), gate(2),
                  wbr(w_o_mla), wbr(w_conv_out), wbr(w_o_mem)],
        out_specs=pl.BlockSpec((bm, bn), lambda j, i: (i, j)),
        out_shape=jax.ShapeDtypeStruct((t, d), BF16),
        compiler_params=_params(("parallel", "parallel")),
        name="gated_merge",
    )(x_bf, o_mla, u_conv, o_mem, w_in_p, w_in_p, w_in_p, w_o_mla, w_conv_out, w_o_mem)


def _route_rows(logits, b_router):
    scores = jax.nn.sigmoid(logits[:N_EXPERTS])
    biased = scores + b_router[:N_EXPERTS]
    sc = [scores[e:e + 1] for e in range(N_EXPERTS)]
    bg = [biased[e:e + 1] for e in range(N_EXPERTS)]
    group_score = []
    for g in range(N_GROUPS):
        v = bg[g * EXPERTS_PER_GROUP:(g + 1) * EXPERTS_PER_GROUP]
        best = None
        for a in range(EXPERTS_PER_GROUP):
            for b in range(a + 1, EXPERTS_PER_GROUP):
                pair = v[a] + v[b]
                best = pair if best is None else jnp.maximum(best, pair)
        group_score.append(best)
    best, best_g = group_score[0], jnp.zeros_like(group_score[0], dtype=jnp.int32)
    for g in range(1, N_GROUPS):
        better = group_score[g] > best
        best = jnp.where(better, group_score[g], best)
        best_g = jnp.where(better, g, best_g)
    sel, picked = [], []
    for e in range(N_EXPERTS):
        g, j = divmod(e, EXPERTS_PER_GROUP)
        rank = jnp.zeros_like(best_g)
        for jj in range(EXPERTS_PER_GROUP):
            if jj == j:
                continue
            other = bg[g * EXPERTS_PER_GROUP + jj]
            ahead = (other > bg[e]) | ((other == bg[e]) & (jj < j))
            rank = rank + ahead.astype(jnp.int32)
        s = (best_g == g) & (rank < TOP_K)
        sel.append(s)
        picked.append(jnp.where(s, sc[e], 0.0))
    denom = picked[0]
    for e in range(1, N_EXPERTS):
        denom = denom + picked[e]
    return [p / denom for p in picked], sel


def _outproj_ln_kernel(z_ref, w_ref, x_ref, g_ref, b_ref, wr_ref, br_ref,
                       xo_ref, route_ref, acc_ref, *, alpha, nj, bn):
    j = pl.program_id(1)
    acc_ref[j] = alpha * x_ref[...] + _dot(z_ref[...], w_ref[...])

    @pl.when(j == nj - 1)
    def _():
        d = nj * bn
        tot = jnp.sum(acc_ref[0], -1, keepdims=True)
        for jj in range(1, nj):
            tot = tot + jnp.sum(acc_ref[jj], -1, keepdims=True)
        mu = tot / d
        sq = None
        for jj in range(nj):
            dv = acc_ref[jj] - mu
            part = jnp.sum(dv * dv, -1, keepdims=True)
            sq = part if sq is None else sq + part
        rstd = lax.rsqrt(sq / d + LN_EPS)
        logits = None
        for jj in range(nj):
            lo = jj * bn
            v = (acc_ref[jj] - mu) * rstd * g_ref[:, lo:lo + bn] + b_ref[:, lo:lo + bn]
            xo_ref[:, lo:lo + bn] = v
            part = _dot_nt(wr_ref[:, lo:lo + bn], v.astype(BF16))
            logits = part if logits is None else logits + part
        cw, sel = _route_rows(logits, br_ref[...])
        for e in range(N_EXPERTS):
            route_ref[e:e + 1, :] = cw[e]
            route_ref[N_EXPERTS + e:N_EXPERTS + e + 1, :] = sel[e].astype(F32)


def _outproj_ln_route(l, z, w_out, x_res, ln_g, ln_b, w_router_t, b_router_col, alpha):
    t, d = z.shape
    bm, bn = _blk(t, 512), _blk(d, 512)
    nj = d // bn
    fixed = lambda i, j: (0, 0)
    layer = lambda i, j: (l, 0, 0)
    return pl.pallas_call(
        functools.partial(_outproj_ln_kernel, alpha=alpha, nj=nj, bn=bn),
        grid=(t // bm, nj),
        in_specs=[
            pl.BlockSpec((bm, d), lambda i, j: (i, 0)),
            pl.BlockSpec((None, d, bn), lambda i, j: (l, 0, j)),
            pl.BlockSpec((bm, bn), lambda i, j: (i, j)),
            _resident((None, 1, d), layer),
            _resident((None, 1, d), layer),
            _resident((LANES, d), fixed),
            _resident((LANES, 1), fixed),
        ],
        out_specs=[
            pl.BlockSpec((bm, d), lambda i, j: (i, 0)),
            pl.BlockSpec((ROUTE_ROWS, bm), lambda i, j: (0, i)),
        ],
        out_shape=[
            jax.ShapeDtypeStruct((t, d), F32),
            jax.ShapeDtypeStruct((ROUTE_ROWS, t), F32),
        ],
        scratch_shapes=[pltpu.VMEM((nj, bm, bn), F32)],
        compiler_params=_params(("parallel", "arbitrary")),
        name="outproj_ln_route",
    )(z, w_out, x_res, ln_g, ln_b, w_router_t, b_router_col)


def _row_copy(src_hbm, row, dst_ref, r, sem):
    return pltpu.make_async_copy(src_hbm.at[pl.ds(row, 1), :], dst_ref.at[pl.ds(r, 1), :], sem)


def _ffn_kernel(be_ref, nused_ref, tok0_ref, tok1_ref, x_hbm, wg_ref, wu_ref, wd_ref, y_ref, xbuf, sems, *, rows):
    i = pl.program_id(0)
    n_used = nused_ref[0]
    slot = i % 2

    def gather(tok_ref, s):
        for r in range(rows):
            _row_copy(x_hbm, tok_ref[0, 0, r], xbuf.at[s], r, sems.at[s]).start()

    @pl.when(i == 0)
    def _():
        gather(tok0_ref, 0)

    @pl.when(i >= n_used)
    def _():
        y_ref[...] = jnp.zeros(y_ref.shape, y_ref.dtype)

    @pl.when(i < n_used)
    def _():
        for r in range(rows):
            _row_copy(x_hbm, 0, xbuf.at[slot], r, sems.at[slot]).wait()

        @pl.when(i + 1 < n_used)
        def _():
            gather(tok1_ref, 1 - slot)

        xb = xbuf[slot].astype(BF16)
        g = _dot(xb, wg_ref[...])
        h = (g * jax.nn.sigmoid(g)) * _dot(xb, wu_ref[...])
        y_ref[...] = _dot(h.astype(BF16), wd_ref[...])


def _expert_ffn(l, x, slot_tok, block_e, n_used, w_gate, w_up, w_down, rows):
    p = slot_tok.shape[0]
    d = x.shape[1]
    de = w_gate.shape[3]
    nb = p // rows
    tok = slot_tok.reshape(nb, 1, rows)
    wsel = lambda i, be, nu: (l, be[jnp.minimum(i, nu[0] - 1)], 0, 0)
    return pl.pallas_call(
        functools.partial(_ffn_kernel, rows=rows),
        grid_spec=pltpu.PrefetchScalarGridSpec(
            num_scalar_prefetch=2,
            grid=(nb,),
            in_specs=[
                pl.BlockSpec((1, 1, rows), lambda i, be, nu: (0, 0, 0), memory_space=pltpu.SMEM),
                pl.BlockSpec((1, 1, rows), lambda i, be, nu: (jnp.minimum(i + 1, nb - 1), 0, 0),
                             memory_space=pltpu.SMEM),
                pl.BlockSpec(memory_space=pl.ANY),
                pl.BlockSpec((None, None, d, de), wsel),
                pl.BlockSpec((None, None, d, de), wsel),
                pl.BlockSpec((None, None, de, d), wsel),
            ],
            out_specs=pl.BlockSpec((rows, d), lambda i, be, nu: (i, 0)),
            scratch_shapes=[pltpu.VMEM((2, rows, d), F32), pltpu.SemaphoreType.DMA((2,))],
        ),
        out_shape=jax.ShapeDtypeStruct((p, d), F32),
        compiler_params=_params(("arbitrary",)),
        name="moe_expert_ffn",
    )(block_e, n_used, tok, tok, x, w_gate, w_up, w_down)


def _combine_kernel(pos0_ref, pos1_ref, ys_hbm, x_ref, wts_ref, g_ref, b_ref, xo_ref, xbf_ref, buf_a, buf_b, sems,
                    *, alpha, rows, nb):
    i = pl.program_id(0)
    slot = i % 2

    def gather(pos_ref, s):
        for r in range(rows):
            _row_copy(ys_hbm, pos_ref[0, 0, r], buf_a.at[s], r, sems.at[s]).start()
            _row_copy(ys_hbm, pos_ref[0, 1, r], buf_b.at[s], r, sems.at[s]).start()

    @pl.when(i == 0)
    def _():
        gather(pos0_ref, 0)

    for r in range(rows):
        _row_copy(ys_hbm, 0, buf_a.at[slot], r, sems.at[slot]).wait()
        _row_copy(ys_hbm, 0, buf_b.at[slot], r, sems.at[slot]).wait()

    @pl.when(i + 1 < nb)
    def _():
        gather(pos1_ref, 1 - slot)

    y = wts_ref[:, 0:1] * buf_a[slot] + wts_ref[:, 1:2] * buf_b[slot]
    v = _layer_norm_rows(alpha * x_ref[...] + y, g_ref[...], b_ref[...])
    xo_ref[...] = v
    xbf_ref[...] = v.astype(BF16)


def _combine_ln(l, ys, pos, wts, x_res, ln_g, ln_b, alpha):
    t, d = x_res.shape
    rows = _blk(t, 128)
    nb = t // rows
    layer = lambda i: (l, 0, 0)
    pos_blocks = pos.reshape(nb, rows, 2).transpose(0, 2, 1)
    return pl.pallas_call(
        functools.partial(_combine_kernel, alpha=alpha, rows=rows, nb=nb),
        grid=(nb,),
        in_specs=[
            pl.BlockSpec((1, 2, rows), lambda i: (0, 0, 0), memory_space=pltpu.SMEM),
            pl.BlockSpec((1, 2, rows), lambda i: (jnp.minimum(i + 1, nb - 1), 0, 0), memory_space=pltpu.SMEM),
            pl.BlockSpec(memory_space=pl.ANY),
            pl.BlockSpec((rows, d), lambda i: (i, 0)),
            pl.BlockSpec((rows, LANES), lambda i: (i, 0)),
            _resident((None, 1, d), layer),
            _resident((None, 1, d), layer),
        ],
        out_specs=[pl.BlockSpec((rows, d), lambda i: (i, 0)), pl.BlockSpec((rows, d), lambda i: (i, 0))],
        out_shape=[jax.ShapeDtypeStruct((t, d), F32), jax.ShapeDtypeStruct((t, d), BF16)],
        scratch_shapes=[pltpu.VMEM((2, rows, d), F32), pltpu.VMEM((2, rows, d), F32),
                        pltpu.SemaphoreType.DMA((2,))],
        compiler_params=_params(("arbitrary",)),
        name="moe_combine_ln",
    )(pos_blocks, pos_blocks, ys, x_res, wts, ln_g, ln_b)


def _dispatch_plan(route_t, rows):
    t = route_t.shape[1]
    cw = route_t[:N_EXPERTS].T
    sel = route_t[N_EXPERTS:].T > 0.5
    sel_i = sel.astype(jnp.int32)
    csum = jnp.cumsum(sel_i, axis=0)
    counts = csum[-1]
    padded = (counts + rows - 1) // rows * rows
    seg_end = jnp.cumsum(padded)
    dest = (seg_end - padded)[None, :] + (csum - sel_i)
    nb = t * TOP_K // rows + N_EXPERTS
    n_slots = nb * rows
    pos0 = jnp.min(jnp.where(sel, dest, n_slots), axis=1)
    pos1 = jnp.max(jnp.where(sel, dest, -1), axis=1)
    w0 = jnp.sum(jnp.where(sel & (dest == pos0[:, None]), cw, 0.0), axis=1)
    w1 = jnp.sum(jnp.where(sel & (dest == pos1[:, None]), cw, 0.0), axis=1)
    tok = jnp.arange(t, dtype=jnp.int32)
    slot_tok = jnp.zeros((n_slots,), jnp.int32).at[jnp.concatenate([pos0, pos1])].set(
        jnp.concatenate([tok, tok]), unique_indices=True)
    block_start = jnp.arange(nb, dtype=jnp.int32) * rows
    block_e = jnp.minimum(jnp.sum((seg_end[None, :] <= block_start[:, None]).astype(jnp.int32), axis=1),
                          N_EXPERTS - 1).astype(jnp.int32)
    n_used = (seg_end[-1] // rows).astype(jnp.int32).reshape(1)
    pos = jnp.stack([pos0, pos1], axis=1).astype(jnp.int32)
    wts = jnp.zeros((t, LANES), F32).at[:, 0].set(w0).at[:, 1].set(w1)
    return slot_tok, block_e, n_used, pos, wts


def _rope_tables(positions):
    inv = 1.0 / (ROPE_THETA ** (jnp.arange(0, QK_ROPE, 2, dtype=F32) / QK_ROPE))
    ang = positions.astype(F32).reshape(-1, 1) * inv
    cos, sin = jnp.cos(ang), jnp.sin(ang)
    zero = jnp.zeros_like(cos)
    pad = jnp.zeros((cos.shape[0], LANES - QK_ROPE), F32)
    c_tab = jnp.concatenate([cos, cos, pad], 1)
    s1_tab = jnp.concatenate([zero, sin, pad], 1)
    s2_tab = jnp.concatenate([-sin, zero, pad], 1)
    return c_tab, s1_tab, s2_tab


def kernel(x, mem, positions, w_in, q_norm_g, kv_norm_g, w_uq, w_ukv, w_o_mla, conv_w, conv_b, conv_ln_g, conv_ln_b, w_conv_out, w_mem_kv, w_o_mem, w_out, ln1_g, ln1_b, w_router, b_router, w_gate, w_up, w_down, ln2_g, ln2_b):
    batch, seq, d = x.shape
    depth = w_in.shape[0]
    t = batch * seq
    q_lora, kv_lora, conv_dim = q_norm_g.shape[1], kv_norm_g.shape[1], conv_w.shape[2]
    mem_w = MEM_HEADS * MEM_HEAD_DIM
    assert conv_dim == mem_w
    alpha = (2 * depth) ** 0.25
    moe_rows = 256

    c_tab, s1_tab, s2_tab = _rope_tables(positions)
    mem_bf = mem.reshape(-1, d).astype(BF16)
    w_router_t = jnp.pad(w_router.T, ((0, LANES - N_EXPERTS), (0, 0))).astype(BF16)
    b_router_col = jnp.pad(b_router.astype(F32), (0, LANES - N_EXPERTS)).reshape(LANES, 1)

    o_conv = q_lora + kv_lora + QK_ROPE
    wa_cols = _round_up(o_conv + LANES - QK_ROPE, W_IN_ALIGN)
    w_in_p = jnp.concatenate(
        [w_in[:, :, :o_conv].astype(BF16), jnp.zeros((depth, d, wa_cols - o_conv), BF16),
         w_in[:, :, o_conv:].astype(BF16)], axis=2)
    col_a, col_g, col_m = wa_cols, wa_cols + conv_dim, wa_cols + 2 * conv_dim
    col_gate = col_m + mem_w

    wuq = jnp.pad(w_uq.reshape(depth, q_lora, MLA_HEADS, QK_NOPE + QK_ROPE),
                  ((0, 0), (0, 0), (0, 0), (0, Q_HEAD_PAD - QK_NOPE - QK_ROPE))
                  ).reshape(depth, q_lora, -1).astype(BF16)
    wukv, w_o_mla_b, w_conv_out_b = w_ukv.astype(BF16), w_o_mla.astype(BF16), w_conv_out.astype(BF16)
    w_mem_kv_b, w_o_mem_b, w_out_b = w_mem_kv.astype(BF16), w_o_mem.astype(BF16), w_out.astype(BF16)
    w_gate_b, w_up_b, w_down_b = w_gate.astype(BF16), w_up.astype(BF16), w_down.astype(BF16)
    conv_w_p = jnp.pad(conv_w, ((0, 0), (0, CONV_HALO - CONV_WIDTH), (0, 0)))
    vec = lambda a: a[:, None, :]
    qg, kvg, conv_b3, conv_g3, conv_beta3 = vec(q_norm_g), vec(kv_norm_g), vec(conv_b), vec(conv_ln_g), vec(conv_ln_b)
    ln1_g3, ln1_b3, ln2_g3, ln2_b3 = vec(ln1_g), vec(ln1_b), vec(ln2_g), vec(ln2_b)

    xf = x.reshape(t, d)
    xb = xf.astype(BF16)
    for l in range(depth):
        q, kv, kr = _mla_proj(l, xb, w_in_p, wa_cols, wuq, wukv, qg, kvg, c_tab, s1_tab, s2_tab)
        o_mla = _attention(q.reshape(batch, seq, -1), kv.reshape(batch, seq, -1),
                           kr.reshape(batch, seq, LANES), batch, seq).reshape(t, -1)

        u0, mq = _xproj(l, xb, w_in_p, col_a, col_g, col_m, conv_dim)
        u_conv = _conv_module(l, u0.reshape(batch, seq, conv_dim), conv_w_p, conv_b3, conv_g3, conv_beta3,
                              batch, seq).reshape(t, -1)

        kvm = _matmul(l, mem_bf, w_mem_kv_b, BF16)
        o_mem = _mem_attention(mq.reshape(batch, seq, mem_w), kvm.reshape(batch, -1, 2 * mem_w),
                               batch, seq).reshape(t, -1)

        z = _merge(l, xb, o_mla, u_conv, o_mem, w_in_p, col_gate, w_o_mla_b, w_conv_out_b, w_o_mem_b)
        x1, route_t = _outproj_ln_route(l, z, w_out_b, xf, ln1_g3, ln1_b3, w_router_t, b_router_col, alpha)

        slot_tok, block_e, n_used, pos, wts = _dispatch_plan(route_t, moe_rows)
        ys = _expert_ffn(l, x1, slot_tok, block_e, n_used, w_gate_b, w_up_b, w_down_b, moe_rows)
        xf, xb = _combine_ln(l, ys, pos, wts, x1, ln2_g3, ln2_b3, alpha)
    return xf.reshape(batch, seq, d)
```

```python
import functools

import jax
import jax.numpy as jnp
from jax import lax
from jax.experimental import pallas as pl
from jax.experimental.pallas import tpu as pltpu

F32 = jnp.float32
BF16 = jnp.bfloat16

CHUNK = 64
MLA_HEADS = 16
QK_NOPE = 128
QK_ROPE = 64
V_DIM = 128
ROPE_THETA = 10000.0
MLA_SCALE = (QK_NOPE + QK_ROPE) ** -0.5
CONV_WIDTH = 31
MEM_HEADS = 4
MEM_HEAD_DIM = 256
MEM_SCALE = MEM_HEAD_DIM ** -0.5
N_EXPERTS = 16
N_GROUPS = 4
EXPERTS_PER_GROUP = N_EXPERTS // N_GROUPS
TOP_K = 2
LN_EPS = 1e-5
RMS_EPS = 1e-6

LANES = 128
SUBLANES = 8
Q_HEAD_PAD = 2 * LANES
CONV_HALO = 32
ROUTE_ROWS = 2 * N_EXPERTS
VMEM_LIMIT = 56 * 1024 * 1024
NEG_BIG = -1e30


def _params(dims):
    return pltpu.CompilerParams(dimension_semantics=dims, vmem_limit_bytes=VMEM_LIMIT)


def _blk(dim, pref):
    b = min(dim, pref)
    assert dim % b == 0, (dim, pref)
    return b


def _resident(shape, index_map):
    return pl.BlockSpec(shape, index_map, pipeline_mode=pl.Buffered(1))


def _dot(a, b):
    return jnp.dot(a, b, preferred_element_type=F32)


def _dot_nt(a, b):
    return lax.dot_general(a, b, (((1,), (1,)), ((), ())), preferred_element_type=F32)


def _layer_norm_rows(v, g, b):
    mu = jnp.mean(v, -1, keepdims=True)
    d = v - mu
    var = jnp.mean(d * d, -1, keepdims=True)
    return d * lax.rsqrt(var + LN_EPS) * g + b


def _rope_lanes(t, c, s1, s2):
    return t * c + pltpu.roll(t, 32, 1) * s1 + pltpu.roll(t, 96, 1) * s2


def _mla_proj_kernel(x_ref, wa_ref, wuq_ref, wukv_ref, qg_ref, kvg_ref, c_ref, s1_ref, s2_ref,
                     q_ref, kv_ref, kr_ref, *, q_lora, kv_lora):
    h = _dot(x_ref[...], wa_ref[...])
    cq = h[:, :q_lora]
    ckv = h[:, q_lora:q_lora + kv_lora]
    kr = h[:, q_lora + kv_lora:q_lora + kv_lora + LANES]

    def rms(t, g):
        return t * lax.rsqrt(jnp.mean(t * t, -1, keepdims=True) + RMS_EPS) * g

    c, s1, s2 = c_ref[...], s1_ref[...], s2_ref[...]
    kr_ref[...] = _rope_lanes(kr, c, s1, s2).astype(BF16)
    kv_ref[...] = _dot(rms(ckv, kvg_ref[...]).astype(BF16), wukv_ref[...]).astype(BF16)
    q = _dot(rms(cq, qg_ref[...]).astype(BF16), wuq_ref[...])
    for hh in range(MLA_HEADS):
        lo = hh * Q_HEAD_PAD
        q_ref[:, lo:lo + LANES] = q[:, lo:lo + LANES].astype(BF16)
        q_ref[:, lo + LANES:lo + Q_HEAD_PAD] = _rope_lanes(
            q[:, lo + LANES:lo + Q_HEAD_PAD], c, s1, s2).astype(BF16)


def _mla_proj(l, x_bf, w_in_p, wa_cols, wuq, wukv, qg, kvg, c_tab, s1_tab, s2_tab):
    t, d = x_bf.shape
    q_lora, kv_lora = qg.shape[2], kvg.shape[2]
    nq, nkv = wuq.shape[2], wukv.shape[2]
    bm = _blk(t, 256)
    row = lambda i: (i, 0)
    layer = lambda i: (l, 0, 0)
    return pl.pallas_call(
        functools.partial(_mla_proj_kernel, q_lora=q_lora, kv_lora=kv_lora),
        grid=(t // bm,),
        in_specs=[
            pl.BlockSpec((bm, d), row),
            _resident((None, d, wa_cols), layer),
            _resident((None, q_lora, nq), layer),
            _resident((None, kv_lora, nkv), layer),
            _resident((None, 1, q_lora), layer),
            _resident((None, 1, kv_lora), layer),
            pl.BlockSpec((bm, LANES), row),
            pl.BlockSpec((bm, LANES), row),
            pl.BlockSpec((bm, LANES), row),
        ],
        out_specs=[
            pl.BlockSpec((bm, nq), row),
            pl.BlockSpec((bm, nkv), row),
            pl.BlockSpec((bm, LANES), row),
        ],
        out_shape=[
            jax.ShapeDtypeStruct((t, nq), BF16),
            jax.ShapeDtypeStruct((t, nkv), BF16),
            jax.ShapeDtypeStruct((t, LANES), BF16),
        ],
        compiler_params=_params(("parallel",)),
        name="mla_proj",
    )(x_bf, w_in_p, wuq, wukv, qg, kvg, c_tab, s1_tab, s2_tab)


def _attn_kernel(q_ref, kv_ref, kr_ref, o_ref, kcat_ref, *, seq, tq):
    kcat_ref[:, :LANES] = kv_ref[0, :, :LANES]
    kcat_ref[:, LANES:] = kr_ref[0]
    rq = lax.broadcasted_iota(jnp.int32, (tq, tq), 0) // CHUNK
    ck = lax.broadcasted_iota(jnp.int32, (tq, tq), 1) // CHUNK
    diag_ok = ck <= rq
    for i in range(seq // tq):
        lo = i * tq
        q = q_ref[0, lo:lo + tq, :]
        s_d = jnp.where(diag_ok, _dot_nt(q, kcat_ref[lo:lo + tq, :]) * MLA_SCALE, NEG_BIG)
        m = jnp.max(s_d, -1, keepdims=True)
        if i > 0:
            s_f = _dot_nt(q, kcat_ref[:lo, :]) * MLA_SCALE
            m = jnp.maximum(m, jnp.max(s_f, -1, keepdims=True))
        p_d = jnp.exp(s_d - m)
        l = jnp.sum(p_d, -1, keepdims=True)
        o = _dot(p_d.astype(BF16), kv_ref[0, lo:lo + tq, LANES:])
        if i > 0:
            p_f = jnp.exp(s_f - m)
            l = l + jnp.sum(p_f, -1, keepdims=True)
            o = o + _dot(p_f.astype(BF16), kv_ref[0, :lo, LANES:])
        o_ref[0, lo:lo + tq, :] = (o / l).astype(BF16)


def _attention(q, kv, kr, batch, seq):
    tq = _blk(seq, 256)
    return pl.pallas_call(
        functools.partial(_attn_kernel, seq=seq, tq=tq),
        grid=(batch, MLA_HEADS),
        in_specs=[
            pl.BlockSpec((1, seq, Q_HEAD_PAD), lambda b, h: (b, 0, h)),
            pl.BlockSpec((1, seq, QK_NOPE + V_DIM), lambda b, h: (b, 0, h)),
            pl.BlockSpec((1, seq, LANES), lambda b, h: (b, 0, 0)),
        ],
        out_specs=pl.BlockSpec((1, seq, V_DIM), lambda b, h: (b, 0, h)),
        out_shape=jax.ShapeDtypeStruct((batch, seq, MLA_HEADS * V_DIM), BF16),
        scratch_shapes=[pltpu.VMEM((seq, Q_HEAD_PAD), BF16)],
        compiler_params=_params(("parallel", "parallel")),
        name="mla_attention",
    )(q, kv, kr)


def _xproj_kernel(x_ref, wa_ref, wg_ref, wm_ref, u_ref, mq_ref):
    x = x_ref[...]
    u_ref[...] = _dot(x, wa_ref[...]) * jax.nn.sigmoid(_dot(x, wg_ref[...]))
    mq_ref[...] = _dot(x, wm_ref[...]).astype(BF16)


def _xproj(l, x_bf, w_in_p, col_a, col_g, col_m, n):
    t, d = x_bf.shape
    bm, bn = _blk(t, 512), _blk(n, 512)
    wspec = lambda col: pl.BlockSpec((None, d, bn), lambda j, i: (l, 0, col // bn + j))
    ospec = pl.BlockSpec((bm, bn), lambda j, i: (i, j))
    return pl.pallas_call(
        _xproj_kernel,
        grid=(n // bn, t // bm),
        in_specs=[pl.BlockSpec((bm, d), lambda j, i: (i, 0)), wspec(col_a), wspec(col_g), wspec(col_m)],
        out_specs=[ospec, ospec],
        out_shape=[jax.ShapeDtypeStruct((t, n), F32), jax.ShapeDtypeStruct((t, n), BF16)],
        compiler_params=_params(("parallel", "parallel")),
        name="xproj_glu_memq",
    )(x_bf, w_in_p, w_in_p, w_in_p)


def _mm_kernel(a_ref, w_ref, o_ref):
    o_ref[...] = _dot(a_ref[...], w_ref[...]).astype(o_ref.dtype)


def _matmul(l, a, w, out_dtype):
    m, k = a.shape
    n = w.shape[2]
    bm, bn = _blk(m, 512), _blk(n, 512)
    return pl.pallas_call(
        _mm_kernel,
        grid=(n // bn, m // bm),
        in_specs=[pl.BlockSpec((bm, k), lambda j, i: (i, 0)), pl.BlockSpec((None, k, bn), lambda j, i: (l, 0, j))],
        out_specs=pl.BlockSpec((bm, bn), lambda j, i: (i, j)),
        out_shape=jax.ShapeDtypeStruct((m, n), out_dtype),
        compiler_params=_params(("parallel", "parallel")),
        name="matmul",
    )(a, w)


def _conv_kernel(cur_ref, prev_ref, w_ref, b_ref, g_ref, beta_ref, o_ref, win_ref, *, ts, rt):
    i = pl.program_id(1)
    win_ref[0:CONV_HALO, :] = jnp.where(i > 0, prev_ref[0, ts - CONV_HALO:ts, :], 0.0)
    win_ref[CONV_HALO:CONV_HALO + ts, :] = cur_ref[0]
    win_ref[CONV_HALO + ts:, :] = jnp.zeros((SUBLANES, win_ref.shape[1]), F32)
    first_tap = CONV_HALO - (CONV_WIDTH - 1)
    for r in range(ts // rt):
        r0 = r * rt
        acc = None
        for phase in range(SUBLANES):
            ys = None
            for k in range(CONV_WIDTH):
                off = first_tap + k
                if off % SUBLANES != phase:
                    continue
                base = r0 + off - phase
                term = win_ref[base:base + rt + SUBLANES, :] * w_ref[k:k + 1, :]
                ys = term if ys is None else ys + term
            part = ys[phase:phase + rt, :]
            acc = part if acc is None else acc + part
        v = _layer_norm_rows(acc + b_ref[...], g_ref[...], beta_ref[...])
        o_ref[0, r0:r0 + rt, :] = (v * jax.nn.sigmoid(v)).astype(BF16)


def _conv_module(l, u0, conv_w, conv_b, ln_g, ln_b, batch, seq):
    c = u0.shape[-1]
    ts = _blk(seq, 256)
    assert ts >= CONV_HALO
    rt = _blk(ts, 32)
    vec = lambda b, i: (l, 0, 0)
    return pl.pallas_call(
        functools.partial(_conv_kernel, ts=ts, rt=rt),
        grid=(batch, seq // ts),
        in_specs=[
            pl.BlockSpec((1, ts, c), lambda b, i: (b, i, 0)),
            pl.BlockSpec((1, ts, c), lambda b, i: (b, jnp.maximum(i - 1, 0), 0)),
            pl.BlockSpec((None, CONV_HALO, c), vec),
            pl.BlockSpec((None, 1, c), vec),
            pl.BlockSpec((None, 1, c), vec),
            pl.BlockSpec((None, 1, c), vec),
        ],
        out_specs=pl.BlockSpec((1, ts, c), lambda b, i: (b, i, 0)),
        out_shape=jax.ShapeDtypeStruct((batch, seq, c), BF16),
        scratch_shapes=[pltpu.VMEM((CONV_HALO + ts + SUBLANES, c), F32)],
        compiler_params=_params(("parallel", "arbitrary")),
        name="conv_module",
    )(u0, u0, conv_w, conv_b, ln_g, ln_b)


def _mem_attn_kernel(q_ref, kv_ref, o_ref):
    width = MEM_HEADS * MEM_HEAD_DIM
    for h in range(MEM_HEADS):
        lo = h * MEM_HEAD_DIM
        s = _dot_nt(q_ref[0, :, lo:lo + MEM_HEAD_DIM], kv_ref[0, :, lo:lo + MEM_HEAD_DIM]) * MEM_SCALE
        p = jnp.exp(s - jnp.max(s, -1, keepdims=True))
        l = jnp.sum(p, -1, keepdims=True)
        o = _dot(p.astype(BF16), kv_ref[0, :, width + lo:width + lo + MEM_HEAD_DIM])
        o_ref[0, :, lo:lo + MEM_HEAD_DIM] = (o / l).astype(BF16)


def _mem_attention(mq, kvm, batch, seq):
    width = MEM_HEADS * MEM_HEAD_DIM
    m = kvm.shape[1]
    ts = _blk(seq, 512)
    return pl.pallas_call(
        _mem_attn_kernel,
        grid=(batch, seq // ts),
        in_specs=[
            pl.BlockSpec((1, ts, width), lambda b, i: (b, i, 0)),
            pl.BlockSpec((1, m, 2 * width), lambda b, i: (b, 0, 0)),
        ],
        out_specs=pl.BlockSpec((1, ts, width), lambda b, i: (b, i, 0)),
        out_shape=jax.ShapeDtypeStruct((batch, seq, width), BF16),
        compiler_params=_params(("parallel", "parallel")),
        name="mem_attention",
    )(mq, kvm)


def _merge_kernel(x_ref, o_ref, u_ref, m_ref, wg0_ref, wg1_ref, wg2_ref, w0_ref, w1_ref, w2_ref, z_ref):
    x = x_ref[...]
    z = jax.nn.sigmoid(_dot(x, wg0_ref[...])) * _dot(o_ref[...], w0_ref[...])
    z = z + jax.nn.sigmoid(_dot(x, wg1_ref[...])) * _dot(u_ref[...], w1_ref[...])
    z = z + jax.nn.sigmoid(_dot(x, wg2_ref[...])) * _dot(m_ref[...], w2_ref[...])
    z_ref[...] = z.astype(BF16)


def _merge(l, x_bf, o_mla, u_conv, o_mem, w_in_p, col_gate, w_o_mla, w_conv_out, w_o_mem):
    t, d = x_bf.shape
    bm, bn = _blk(t, 512), _blk(d, 256)
    nj = d // bn
    act = lambda a: pl.BlockSpec((bm, a.shape[1]), lambda j, i: (i, 0))
    gate = lambda b: pl.BlockSpec((None, d, bn), lambda j, i, b=b: (l, 0, col_gate // bn + b * nj + j))
    wbr = lambda w: pl.BlockSpec((None, w.shape[1], bn), lambda j, i: (l, 0, j))
    return pl.pallas_call(
        _merge_kernel,
        grid=(nj, t // bm),
        in_specs=[act(x_bf), act(o_mla), act(u_conv), act(o_mem), gate(0), gate(1), gate(2),
                  wbr(w_o_mla), wbr(w_conv_out), wbr(w_o_mem)],
        out_specs=pl.BlockSpec((bm, bn), lambda j, i: (i, j)),
        out_shape=jax.ShapeDtypeStruct((t, d), BF16),
        compiler_params=_params(("parallel", "parallel")),
        name="gated_merge",
    )(x_bf, o_mla, u_conv, o_mem, w_in_p, w_in_p, w_in_p, w_o_mla, w_conv_out, w_o_mem)


def _route_rows(logits, b_router):
    scores = jax.nn.sigmoid(logits[:N_EXPERTS])
    biased = scores + b_router[:N_EXPERTS]
    sc = [scores[e:e + 1] for e in range(N_EXPERTS)]
    bg = [biased[e:e + 1] for e in range(N_EXPERTS)]
    group_score = []
    for g in range(N_GROUPS):
        v = bg[g * EXPERTS_PER_GROUP:(g + 1) * EXPERTS_PER_GROUP]
        best = None
        for a in range(EXPERTS_PER_GROUP):
            for b in range(a + 1, EXPERTS_PER_GROUP):
                pair = v[a] + v[b]
                best = pair if best is None else jnp.maximum(best, pair)
        group_score.append(best)
    best, best_g = group_score[0], jnp.zeros_like(group_score[0], dtype=jnp.int32)
    for g in range(1, N_GROUPS):
        better = group_score[g] > best
        best = jnp.where(better, group_score[g], best)
        best_g = jnp.where(better, g, best_g)
    sel, picked = [], []
    for e in range(N_EXPERTS):
        g, j = divmod(e, EXPERTS_PER_GROUP)
        rank = jnp.zeros_like(best_g)
        for jj in range(EXPERTS_PER_GROUP):
            if jj == j:
                continue
            other = bg[g * EXPERTS_PER_GROUP + jj]
            ahead = (other > bg[e]) | ((other == bg[e]) & (jj < j))
            rank = rank + ahead.astype(jnp.int32)
        s = (best_g == g) & (rank < TOP_K)
        sel.append(s)
        picked.append(jnp.where(s, sc[e], 0.0))
    denom = picked[0]
    for e in range(1, N_EXPERTS):
        denom = denom + picked[e]
    return [p / denom for p in picked], sel


def _outproj_ln_kernel(z_ref, w_ref, x_ref, g_ref, b_ref, wr_ref, br_ref,
                       xo_ref, route_ref, acc_ref, *, alpha, nj, bn):
    j = pl.program_id(1)
    acc_ref[j] = alpha * x_ref[...] + _dot(z_ref[...], w_ref[...])

    @pl.when(j == nj - 1)
    def _():
        d = nj * bn
        tot = jnp.sum(acc_ref[0], -1, keepdims=True)
        for jj in range(1, nj):
            tot = tot + jnp.sum(acc_ref[jj], -1, keepdims=True)
        mu = tot / d
        sq = None
        for jj in range(nj):
            dv = acc_ref[jj] - mu
            part = jnp.sum(dv * dv, -1, keepdims=True)
            sq = part if sq is None else sq + part
        rstd = lax.rsqrt(sq / d + LN_EPS)
        logits = None
        for jj in range(nj):
            lo = jj * bn
            v = (acc_ref[jj] - mu) * rstd * g_ref[:, lo:lo + bn] + b_ref[:, lo:lo + bn]
            xo_ref[:, lo:lo + bn] = v
            part = _dot_nt(wr_ref[:, lo:lo + bn], v.astype(BF16))
            logits = part if logits is None else logits + part
        cw, sel = _route_rows(logits, br_ref[...])
        for e in range(N_EXPERTS):
            route_ref[e:e + 1, :] = cw[e]
            route_ref[N_EXPERTS + e:N_EXPERTS + e + 1, :] = sel[e].astype(F32)


def _outproj_ln_route(l, z, w_out, x_res, ln_g, ln_b, w_router_t, b_router_col, alpha):
    t, d = z.shape
    bm, bn = _blk(t, 512), _blk(d, 512)
    nj = d // bn
    fixed = lambda i, j: (0, 0)
    layer = lambda i, j: (l, 0, 0)
    return pl.pallas_call(
        functools.partial(_outproj_ln_kernel, alpha=alpha, nj=nj, bn=bn),
        grid=(t // bm, nj),
        in_specs=[
            pl.BlockSpec((bm, d), lambda i, j: (i, 0)),
            pl.BlockSpec((None, d, bn), lambda i, j: (l, 0, j)),
            pl.BlockSpec((bm, bn), lambda i, j: (i, j)),
            _resident((None, 1, d), layer),
            _resident((None, 1, d), layer),
            _resident((LANES, d), fixed),
            _resident((LANES, 1), fixed),
        ],
        out_specs=[
            pl.BlockSpec((bm, d), lambda i, j: (i, 0)),
            pl.BlockSpec((ROUTE_ROWS, bm), lambda i, j: (0, i)),
        ],
        out_shape=[
            jax.ShapeDtypeStruct((t, d), F32),
            jax.ShapeDtypeStruct((ROUTE_ROWS, t), F32),
        ],
        scratch_shapes=[pltpu.VMEM((nj, bm, bn), F32)],
        compiler_params=_params(("parallel", "arbitrary")),
        name="outproj_ln_route",
    )(z, w_out, x_res, ln_g, ln_b, w_router_t, b_router_col)


def _row_copy(src_hbm, row, dst_ref, r, sem):
    return pltpu.make_async_copy(src_hbm.at[pl.ds(row, 1), :], dst_ref.at[pl.ds(r, 1), :], sem)


def _ffn_kernel(be_ref, nused_ref, tok0_ref, tok1_ref, x_hbm, wg_ref, wu_ref, wd_ref, y_ref,
                xbuf0, xbuf1, sems, *, rows):
    i = pl.program_id(0)
    n_used = nused_ref[0]
    bufs = (xbuf0, xbuf1)

    def gather(tok_ref, s):
        for r in range(rows):
            _row_copy(x_hbm, tok_ref[0, 0, r], bufs[s], r, sems.at[s]).start()

    @pl.when(i == 0)
    def _():
        gather(tok0_ref, 0)

    @pl.when(i >= n_used)
    def _():
        y_ref[...] = jnp.zeros(y_ref.shape, y_ref.dtype)

    for s in range(2):
        mine = i % 2 == s

        @pl.when(mine & (i <= n_used))
        def _():
            for r in range(rows):
                _row_copy(x_hbm, 0, bufs[s], r, sems.at[s]).wait()

        @pl.when(mine & (i < n_used))
        def _():
            gather(tok1_ref, 1 - s)
            xb = bufs[s][...].astype(BF16)
            g = _dot(xb, wg_ref[...])
            h = (g * jax.nn.sigmoid(g)) * _dot(xb, wu_ref[...])
            y_ref[...] = _dot(h.astype(BF16), wd_ref[...])


def _expert_ffn(l, x, slot_tok, block_e, n_used, w_gate, w_up, w_down, rows):
    p = slot_tok.shape[0]
    d = x.shape[1]
    de = w_gate.shape[3]
    nb = p // rows
    tok = slot_tok.reshape(nb, 1, rows)
    wsel = lambda i, be, nu: (l, be[jnp.minimum(i, nu[0] - 1)], 0, 0)
    return pl.pallas_call(
        functools.partial(_ffn_kernel, rows=rows),
        grid_spec=pltpu.PrefetchScalarGridSpec(
            num_scalar_prefetch=2,
            grid=(nb,),
            in_specs=[
                pl.BlockSpec((1, 1, rows), lambda i, be, nu: (0, 0, 0), memory_space=pltpu.SMEM),
                pl.BlockSpec((1, 1, rows), lambda i, be, nu: (jnp.minimum(i + 1, nb - 1), 0, 0),
                             memory_space=pltpu.SMEM),
                pl.BlockSpec(memory_space=pl.ANY),
                pl.BlockSpec((None, None, d, de), wsel),
                pl.BlockSpec((None, None, d, de), wsel),
                pl.BlockSpec((None, None, de, d), wsel),
            ],
            out_specs=pl.BlockSpec((rows, d), lambda i, be, nu: (i, 0)),
            scratch_shapes=[pltpu.VMEM((rows, d), F32), pltpu.VMEM((rows, d), F32), pltpu.SemaphoreType.DMA((2,))],
        ),
        out_shape=jax.ShapeDtypeStruct((p, d), F32),
        compiler_params=_params(("arbitrary",)),
        name="moe_expert_ffn",
    )(block_e, n_used, tok, tok, x, w_gate, w_up, w_down)


def _combine_kernel(pos0_ref, pos1_ref, ys_hbm, x_ref, wts_ref, g_ref, b_ref, xo_ref, xbf_ref,
                    a0, b0, a1, b1, sems, *, alpha, rows, nb):
    i = pl.program_id(0)
    bufs = ((a0, b0), (a1, b1))

    def gather(pos_ref, s):
        for r in range(rows):
            _row_copy(ys_hbm, pos_ref[0, 0, r], bufs[s][0], r, sems.at[s]).start()
            _row_copy(ys_hbm, pos_ref[0, 1, r], bufs[s][1], r, sems.at[s]).start()

    def drain(s):
        for r in range(rows):
            _row_copy(ys_hbm, 0, bufs[s][0], r, sems.at[s]).wait()
            _row_copy(ys_hbm, 0, bufs[s][1], r, sems.at[s]).wait()

    @pl.when(i == 0)
    def _():
        gather(pos0_ref, 0)

    for s in range(2):
        @pl.when(i % 2 == s)
        def _():
            drain(s)
            gather(pos1_ref, 1 - s)
            y = wts_ref[:, 0:1] * bufs[s][0][...] + wts_ref[:, 1:2] * bufs[s][1][...]
            v = _layer_norm_rows(alpha * x_ref[...] + y, g_ref[...], b_ref[...])
            xo_ref[...] = v
            xbf_ref[...] = v.astype(BF16)

        @pl.when((i % 2 == s) & (i == nb - 1))
        def _():
            drain(1 - s)


def _combine_ln(l, ys, pos, wts, x_res, ln_g, ln_b, alpha):
    t, d = x_res.shape
    rows = _blk(t, 128)
    nb = t // rows
    layer = lambda i: (l, 0, 0)
    pos_blocks = pos.reshape(nb, rows, 2).transpose(0, 2, 1)
    return pl.pallas_call(
        functools.partial(_combine_kernel, alpha=alpha, rows=rows, nb=nb),
        grid=(nb,),
        in_specs=[
            pl.BlockSpec((1, 2, rows), lambda i: (0, 0, 0), memory_space=pltpu.SMEM),
            pl.BlockSpec((1, 2, rows), lambda i: (jnp.minimum(i + 1, nb - 1), 0, 0), memory_space=pltpu.SMEM),
            pl.BlockSpec(memory_space=pl.ANY),
            pl.BlockSpec((rows, d), lambda i: (i, 0)),
            pl.BlockSpec((rows, LANES), lambda i: (i, 0)),
            _resident((None, 1, d), layer),
            _resident((None, 1, d), layer),
        ],
        out_specs=[pl.BlockSpec((rows, d), lambda i: (i, 0)), pl.BlockSpec((rows, d), lambda i: (i, 0))],
        out_shape=[jax.ShapeDtypeStruct((t, d), F32), jax.ShapeDtypeStruct((t, d), BF16)],
        scratch_shapes=[pltpu.VMEM((rows, d), F32)] * 4 + [pltpu.SemaphoreType.DMA((2,))],
        compiler_params=_params(("arbitrary",)),
        name="moe_combine_ln",
    )(pos_blocks, pos_blocks, ys, x_res, wts, ln_g, ln_b)


def _dispatch_plan(route_t, rows):
    t = route_t.shape[1]
    cw = route_t[:N_EXPERTS].T
    sel = route_t[N_EXPERTS:].T > 0.5
    sel_i = sel.astype(jnp.int32)
    csum = jnp.cumsum(sel_i, axis=0)
    counts = csum[-1]
    padded = (counts + rows - 1) // rows * rows
    seg_end = jnp.cumsum(padded)
    dest = (seg_end - padded)[None, :] + (csum - sel_i)
    nb = t * TOP_K // rows + N_EXPERTS
    n_slots = nb * rows
    pos0 = jnp.min(jnp.where(sel, dest, n_slots), axis=1)
    pos1 = jnp.max(jnp.where(sel, dest, -1), axis=1)
    w0 = jnp.sum(jnp.where(sel & (dest == pos0[:, None]), cw, 0.0), axis=1)
    w1 = jnp.sum(jnp.where(sel & (dest == pos1[:, None]), cw, 0.0), axis=1)
    tok = jnp.arange(t, dtype=jnp.int32)
    slot_tok = jnp.zeros((n_slots,), jnp.int32).at[jnp.concatenate([pos0, pos1])].set(
        jnp.concatenate([tok, tok]), unique_indices=True)
    block_start = jnp.arange(nb, dtype=jnp.int32) * rows
    block_e = jnp.minimum(jnp.sum((seg_end[None, :] <= block_start[:, None]).astype(jnp.int32), axis=1),
                          N_EXPERTS - 1).astype(jnp.int32)
    n_used = (seg_end[-1] // rows).astype(jnp.int32).reshape(1)
    pos = jnp.stack([pos0, pos1], axis=1).astype(jnp.int32)
    wts = jnp.zeros((t, LANES), F32).at[:, 0].set(w0).at[:, 1].set(w1)
    return slot_tok, block_e, n_used, pos, wts


def _rope_tables(positions):
    inv = 1.0 / (ROPE_THETA ** (jnp.arange(0, QK_ROPE, 2, dtype=F32) / QK_ROPE))
    ang = positions.astype(F32).reshape(-1, 1) * inv
    cos, sin = jnp.cos(ang), jnp.sin(ang)
    zero = jnp.zeros_like(cos)
    pad = jnp.zeros((cos.shape[0], LANES - QK_ROPE), F32)
    c_tab = jnp.concatenate([cos, cos, pad], 1)
    s1_tab = jnp.concatenate([zero, sin, pad], 1)
    s2_tab = jnp.concatenate([-sin, zero, pad], 1)
    return c_tab, s1_tab, s2_tab


def kernel(x, mem, positions, w_in, q_norm_g, kv_norm_g, w_uq, w_ukv, w_o_mla, conv_w, conv_b, conv_ln_g, conv_ln_b, w_conv_out, w_mem_kv, w_o_mem, w_out, ln1_g, ln1_b, w_router, b_router, w_gate, w_up, w_down, ln2_g, ln2_b):
    batch, seq, d = x.shape
    depth = w_in.shape[0]
    t = batch * seq
    q_lora, kv_lora, conv_dim = q_norm_g.shape[1], kv_norm_g.shape[1], conv_w.shape[2]
    mem_w = MEM_HEADS * MEM_HEAD_DIM
    assert conv_dim == mem_w
    alpha = (2 * depth) ** 0.25
    moe_rows = 256

    c_tab, s1_tab, s2_tab = _rope_tables(positions)
    mem_bf = mem.reshape(-1, d).astype(BF16)
    w_router_t = jnp.pad(w_router.T, ((0, LANES - N_EXPERTS), (0, 0))).astype(BF16)
    b_router_col = jnp.pad(b_router.astype(F32), (0, LANES - N_EXPERTS)).reshape(LANES, 1)

    o_conv = q_lora + kv_lora + QK_ROPE
    wa_cols = o_conv + LANES - QK_ROPE
    w_lat = jnp.pad(w_in[:, :, :o_conv].astype(BF16), ((0, 0), (0, 0), (0, wa_cols - o_conv)))
    w_rest = w_in[:, :, o_conv:].astype(BF16)
    col_a, col_g, col_m = 0, conv_dim, 2 * conv_dim
    col_gate = col_m + mem_w

    wuq = jnp.pad(w_uq.reshape(depth, q_lora, MLA_HEADS, QK_NOPE + QK_ROPE),
                  ((0, 0), (0, 0), (0, 0), (0, Q_HEAD_PAD - QK_NOPE - QK_ROPE))
                  ).reshape(depth, q_lora, -1).astype(BF16)
    wukv, w_o_mla_b, w_conv_out_b = w_ukv.astype(BF16), w_o_mla.astype(BF16), w_conv_out.astype(BF16)
    w_mem_kv_b, w_o_mem_b, w_out_b = w_mem_kv.astype(BF16), w_o_mem.astype(BF16), w_out.astype(BF16)
    w_gate_b, w_up_b, w_down_b = w_gate.astype(BF16), w_up.astype(BF16), w_down.astype(BF16)
    conv_w_p = jnp.pad(conv_w, ((0, 0), (0, CONV_HALO - CONV_WIDTH), (0, 0)))
    vec = lambda a: a[:, None, :]
    qg, kvg, conv_b3, conv_g3, conv_beta3 = vec(q_norm_g), vec(kv_norm_g), vec(conv_b), vec(conv_ln_g), vec(conv_ln_b)
    ln1_g3, ln1_b3, ln2_g3, ln2_b3 = vec(ln1_g), vec(ln1_b), vec(ln2_g), vec(ln2_b)

    xf = x.reshape(t, d)
    xb = xf.astype(BF16)
    for l in range(depth):
        q, kv, kr = _mla_proj(l, xb, w_lat, wa_cols, wuq, wukv, qg, kvg, c_tab, s1_tab, s2_tab)
        o_mla = _attention(q.reshape(batch, seq, -1), kv.reshape(batch, seq, -1),
                           kr.reshape(batch, seq, LANES), batch, seq).reshape(t, -1)

        u0, mq = _xproj(l, xb, w_rest, col_a, col_g, col_m, conv_dim)
        u_conv = _conv_module(l, u0.reshape(batch, seq, conv_dim), conv_w_p, conv_b3, conv_g3, conv_beta3,
                              batch, seq).reshape(t, -1)

        kvm = _matmul(l, mem_bf, w_mem_kv_b, BF16)
        o_mem = _mem_attention(mq.reshape(batch, seq, mem_w), kvm.reshape(batch, -1, 2 * mem_w),
                               batch, seq).reshape(t, -1)

        z = _merge(l, xb, o_mla, u_conv, o_mem, w_rest, col_gate, w_o_mla_b, w_conv_out_b, w_o_mem_b)
        x1, route_t = _outproj_ln_route(l, z, w_out_b, xf, ln1_g3, ln1_b3, w_router_t, b_router_col, alpha)

        slot_tok, block_e, n_used, pos, wts = _dispatch_plan(route_t, moe_rows)
        ys = _expert_ffn(l, x1, slot_tok, block_e, n_used, w_gate_b, w_up_b, w_down_b, moe_rows)
        xf, xb = _combine_ln(l, ys, pos, wts, x1, ln2_g3, ln2_b3, alpha)
    return xf.reshape(batch, seq, d)
```

```python
import functools

import jax
import jax.numpy as jnp
from jax import lax
from jax.experimental import pallas as pl
from jax.experimental.pallas import tpu as pltpu

F32 = jnp.float32
BF16 = jnp.bfloat16

CHUNK = 64
MLA_HEADS = 16
QK_NOPE = 128
QK_ROPE = 64
V_DIM = 128
ROPE_THETA = 10000.0
MLA_SCALE = (QK_NOPE + QK_ROPE) ** -0.5
CONV_WIDTH = 31
MEM_HEADS = 4
MEM_HEAD_DIM = 256
MEM_SCALE = MEM_HEAD_DIM ** -0.5
N_EXPERTS = 16
N_GROUPS = 4
EXPERTS_PER_GROUP = N_EXPERTS // N_GROUPS
TOP_K = 2
LN_EPS = 1e-5
RMS_EPS = 1e-6

LANES = 128
SUBLANES = 8
Q_HEAD_PAD = 2 * LANES
CONV_HALO = 32
ROUTE_ROWS = 2 * N_EXPERTS
VMEM_LIMIT = 56 * 1024 * 1024
VMEM_LIMIT_OUTPROJ = 60 * 1024 * 1024
NEG_BIG = -1e30


def _params(dims, vmem_limit=VMEM_LIMIT):
    return pltpu.CompilerParams(dimension_semantics=dims, vmem_limit_bytes=vmem_limit)


def _blk(dim, pref):
    b = min(dim, pref)
    assert dim % b == 0, (dim, pref)
    return b


def _resident(shape, index_map):
    return pl.BlockSpec(shape, index_map, pipeline_mode=pl.Buffered(1))


def _dot(a, b):
    return jnp.dot(a, b, preferred_element_type=F32)


def _dot_nt(a, b):
    return lax.dot_general(a, b, (((1,), (1,)), ((), ())), preferred_element_type=F32)


def _layer_norm_rows(v, g, b):
    mu = jnp.mean(v, -1, keepdims=True)
    d = v - mu
    var = jnp.mean(d * d, -1, keepdims=True)
    return d * lax.rsqrt(var + LN_EPS) * g + b


def _cast_cols_kernel(w_ref, o_ref, *, start):
    o_ref[...] = w_ref[:, start:start + o_ref.shape[1]].astype(BF16)


def _cast_cols(w, start):
    depth, rows, cols = w.shape
    width = cols - start
    rt = _blk(rows, 128)
    return pl.pallas_call(
        functools.partial(_cast_cols_kernel, start=start),
        grid=(depth, rows // rt),
        in_specs=[pl.BlockSpec((None, rt, cols), lambda l, i: (l, i, 0))],
        out_specs=pl.BlockSpec((None, rt, width), lambda l, i: (l, i, 0)),
        out_shape=jax.ShapeDtypeStruct((depth, rows, width), BF16),
        compiler_params=_params(("parallel", "parallel")),
        name="cast_cols",
    )(w)


def _rope_lanes(t, c, s1, s2):
    return t * c + pltpu.roll(t, 32, 1) * s1 + pltpu.roll(t, 96, 1) * s2


def _mla_proj_kernel(x_ref, wa_ref, wuq_ref, wukv_ref, qg_ref, kvg_ref, c_ref, s1_ref, s2_ref,
                     q_ref, kv_ref, kr_ref, *, q_lora, kv_lora):
    h = _dot(x_ref[...], wa_ref[...])
    cq = h[:, :q_lora]
    ckv = h[:, q_lora:q_lora + kv_lora]
    kr = h[:, q_lora + kv_lora:q_lora + kv_lora + LANES]

    def rms(t, g):
        return t * lax.rsqrt(jnp.mean(t * t, -1, keepdims=True) + RMS_EPS) * g

    c, s1, s2 = c_ref[...], s1_ref[...], s2_ref[...]
    kr_ref[...] = _rope_lanes(kr, c, s1, s2).astype(BF16)
    kv_ref[...] = _dot(rms(ckv, kvg_ref[...]).astype(BF16), wukv_ref[...]).astype(BF16)
    q = _dot(rms(cq, qg_ref[...]).astype(BF16), wuq_ref[...])
    for hh in range(MLA_HEADS):
        lo = hh * Q_HEAD_PAD
        q_ref[:, lo:lo + LANES] = q[:, lo:lo + LANES].astype(BF16)
        q_ref[:, lo + LANES:lo + Q_HEAD_PAD] = _rope_lanes(
            q[:, lo + LANES:lo + Q_HEAD_PAD], c, s1, s2).astype(BF16)


def _mla_proj(l, x_bf, w_in_p, wa_cols, wuq, wukv, qg, kvg, c_tab, s1_tab, s2_tab):
    t, d = x_bf.shape
    q_lora, kv_lora = qg.shape[2], kvg.shape[2]
    nq, nkv = wuq.shape[2], wukv.shape[2]
    bm = _blk(t, 256)
    row = lambda i: (i, 0)
    layer = lambda i: (l, 0, 0)
    return pl.pallas_call(
        functools.partial(_mla_proj_kernel, q_lora=q_lora, kv_lora=kv_lora),
        grid=(t // bm,),
        in_specs=[
            pl.BlockSpec((bm, d), row),
            _resident((None, d, wa_cols), layer),
            _resident((None, q_lora, nq), layer),
            _resident((None, kv_lora, nkv), layer),
            _resident((None, 1, q_lora), layer),
            _resident((None, 1, kv_lora), layer),
            pl.BlockSpec((bm, LANES), row),
            pl.BlockSpec((bm, LANES), row),
            pl.BlockSpec((bm, LANES), row),
        ],
        out_specs=[
            pl.BlockSpec((bm, nq), row),
            pl.BlockSpec((bm, nkv), row),
            pl.BlockSpec((bm, LANES), row),
        ],
        out_shape=[
            jax.ShapeDtypeStruct((t, nq), BF16),
            jax.ShapeDtypeStruct((t, nkv), BF16),
            jax.ShapeDtypeStruct((t, LANES), BF16),
        ],
        compiler_params=_params(("parallel",)),
        name="mla_proj",
    )(x_bf, w_in_p, wuq, wukv, qg, kvg, c_tab, s1_tab, s2_tab)


def _attn_kernel(q_ref, kv_ref, kr_ref, o_ref, kcat_ref, *, seq, tq):
    kcat_ref[:, :LANES] = kv_ref[0, :, :LANES]
    kcat_ref[:, LANES:] = kr_ref[0]
    rq = lax.broadcasted_iota(jnp.int32, (tq, tq), 0) // CHUNK
    ck = lax.broadcasted_iota(jnp.int32, (tq, tq), 1) // CHUNK
    diag_ok = ck <= rq
    for i in range(seq // tq):
        lo = i * tq
        q = q_ref[0, lo:lo + tq, :]
        s_d = jnp.where(diag_ok, _dot_nt(q, kcat_ref[lo:lo + tq, :]) * MLA_SCALE, NEG_BIG)
        m = jnp.max(s_d, -1, keepdims=True)
        if i > 0:
            s_f = _dot_nt(q, kcat_ref[:lo, :]) * MLA_SCALE
            m = jnp.maximum(m, jnp.max(s_f, -1, keepdims=True))
        p_d = jnp.exp(s_d - m)
        l = jnp.sum(p_d, -1, keepdims=True)
        o = _dot(p_d.astype(BF16), kv_ref[0, lo:lo + tq, LANES:])
        if i > 0:
            p_f = jnp.exp(s_f - m)
            l = l + jnp.sum(p_f, -1, keepdims=True)
            o = o + _dot(p_f.astype(BF16), kv_ref[0, :lo, LANES:])
        o_ref[0, lo:lo + tq, :] = (o / l).astype(BF16)


def _attention(q, kv, kr, batch, seq):
    tq = _blk(seq, 256)
    return pl.pallas_call(
        functools.partial(_attn_kernel, seq=seq, tq=tq),
        grid=(batch, MLA_HEADS),
        in_specs=[
            pl.BlockSpec((1, seq, Q_HEAD_PAD), lambda b, h: (b, 0, h)),
            pl.BlockSpec((1, seq, QK_NOPE + V_DIM), lambda b, h: (b, 0, h)),
            pl.BlockSpec((1, seq, LANES), lambda b, h: (b, 0, 0)),
        ],
        out_specs=pl.BlockSpec((1, seq, V_DIM), lambda b, h: (b, 0, h)),
        out_shape=jax.ShapeDtypeStruct((batch, seq, MLA_HEADS * V_DIM), BF16),
        scratch_shapes=[pltpu.VMEM((seq, Q_HEAD_PAD), BF16)],
        compiler_params=_params(("parallel", "parallel")),
        name="mla_attention",
    )(q, kv, kr)


def _xproj_kernel(x_ref, wa_ref, wg_ref, wm_ref, u_ref, mq_ref):
    x = x_ref[...]
    u_ref[...] = _dot(x, wa_ref[...]) * jax.nn.sigmoid(_dot(x, wg_ref[...]))
    mq_ref[...] = _dot(x, wm_ref[...]).astype(BF16)


def _xproj(l, x_bf, w_in_p, col_a, col_g, col_m, n):
    t, d = x_bf.shape
    bm, bn = _blk(t, 512), _blk(n, 512)
    wspec = lambda col: pl.BlockSpec((None, d, bn), lambda j, i: (l, 0, col // bn + j))
    ospec = pl.BlockSpec((bm, bn), lambda j, i: (i, j))
    return pl.pallas_call(
        _xproj_kernel,
        grid=(n // bn, t // bm),
        in_specs=[pl.BlockSpec((bm, d), lambda j, i: (i, 0)), wspec(col_a), wspec(col_g), wspec(col_m)],
        out_specs=[ospec, ospec],
        out_shape=[jax.ShapeDtypeStruct((t, n), F32), jax.ShapeDtypeStruct((t, n), BF16)],
        compiler_params=_params(("parallel", "parallel")),
        name="xproj_glu_memq",
    )(x_bf, w_in_p, w_in_p, w_in_p)


def _mm_kernel(a_ref, w_ref, o_ref):
    o_ref[...] = _dot(a_ref[...], w_ref[...]).astype(o_ref.dtype)


def _matmul(l, a, w, out_dtype):
    m, k = a.shape
    n = w.shape[2]
    bm, bn = _blk(m, 512), _blk(n, 512)
    return pl.pallas_call(
        _mm_kernel,
        grid=(n // bn, m // bm),
        in_specs=[pl.BlockSpec((bm, k), lambda j, i: (i, 0)), pl.BlockSpec((None, k, bn), lambda j, i: (l, 0, j))],
        out_specs=pl.BlockSpec((bm, bn), lambda j, i: (i, j)),
        out_shape=jax.ShapeDtypeStruct((m, n), out_dtype),
        compiler_params=_params(("parallel", "parallel")),
        name="matmul",
    )(a, w)


def _conv_kernel(cur_ref, prev_ref, w_ref, b_ref, g_ref, beta_ref, o_ref, win_ref, *, ts, rt):
    i = pl.program_id(1)
    win_ref[0:CONV_HALO, :] = jnp.where(i > 0, prev_ref[0, ts - CONV_HALO:ts, :], 0.0)
    win_ref[CONV_HALO:CONV_HALO + ts, :] = cur_ref[0]
    win_ref[CONV_HALO + ts:, :] = jnp.zeros((SUBLANES, win_ref.shape[1]), F32)
    first_tap = CONV_HALO - (CONV_WIDTH - 1)
    for r in range(ts // rt):
        r0 = r * rt
        acc = None
        for phase in range(SUBLANES):
            ys = None
            for k in range(CONV_WIDTH):
                off = first_tap + k
                if off % SUBLANES != phase:
                    continue
                base = r0 + off - phase
                term = win_ref[base:base + rt + SUBLANES, :] * w_ref[k:k + 1, :]
                ys = term if ys is None else ys + term
            part = ys[phase:phase + rt, :]
            acc = part if acc is None else acc + part
        v = _layer_norm_rows(acc + b_ref[...], g_ref[...], beta_ref[...])
        o_ref[0, r0:r0 + rt, :] = (v * jax.nn.sigmoid(v)).astype(BF16)


def _conv_module(l, u0, conv_w, conv_b, ln_g, ln_b, batch, seq):
    c = u0.shape[-1]
    ts = _blk(seq, 256)
    assert ts >= CONV_HALO
    rt = _blk(ts, 32)
    vec = lambda b, i: (l, 0, 0)
    return pl.pallas_call(
        functools.partial(_conv_kernel, ts=ts, rt=rt),
        grid=(batch, seq // ts),
        in_specs=[
            pl.BlockSpec((1, ts, c), lambda b, i: (b, i, 0)),
            pl.BlockSpec((1, ts, c), lambda b, i: (b, jnp.maximum(i - 1, 0), 0)),
            pl.BlockSpec((None, CONV_HALO, c), vec),
            pl.BlockSpec((None, 1, c), vec),
            pl.BlockSpec((None, 1, c), vec),
            pl.BlockSpec((None, 1, c), vec),
        ],
        out_specs=pl.BlockSpec((1, ts, c), lambda b, i: (b, i, 0)),
        out_shape=jax.ShapeDtypeStruct((batch, seq, c), BF16),
        scratch_shapes=[pltpu.VMEM((CONV_HALO + ts + SUBLANES, c), F32)],
        compiler_params=_params(("parallel", "arbitrary")),
        name="conv_module",
    )(u0, u0, conv_w, conv_b, ln_g, ln_b)


def _mem_attn_kernel(q_ref, kv_ref, o_ref):
    width = MEM_HEADS * MEM_HEAD_DIM
    for h in range(MEM_HEADS):
        lo = h * MEM_HEAD_DIM
        s = _dot_nt(q_ref[0, :, lo:lo + MEM_HEAD_DIM], kv_ref[0, :, lo:lo + MEM_HEAD_DIM]) * MEM_SCALE
        p = jnp.exp(s - jnp.max(s, -1, keepdims=True))
        l = jnp.sum(p, -1, keepdims=True)
        o = _dot(p.astype(BF16), kv_ref[0, :, width + lo:width + lo + MEM_HEAD_DIM])
        o_ref[0, :, lo:lo + MEM_HEAD_DIM] = (o / l).astype(BF16)


def _mem_attention(mq, kvm, batch, seq):
    width = MEM_HEADS * MEM_HEAD_DIM
    m = kvm.shape[1]
    ts = _blk(seq, 512)
    return pl.pallas_call(
        _mem_attn_kernel,
        grid=(batch, seq // ts),
        in_specs=[
            pl.BlockSpec((1, ts, width), lambda b, i: (b, i, 0)),
            pl.BlockSpec((1, m, 2 * width), lambda b, i: (b, 0, 0)),
        ],
        out_specs=pl.BlockSpec((1, ts, width), lambda b, i: (b, i, 0)),
        out_shape=jax.ShapeDtypeStruct((batch, seq, width), BF16),
        compiler_params=_params(("parallel", "parallel")),
        name="mem_attention",
    )(mq, kvm)


def _merge_kernel(x_ref, o_ref, u_ref, m_ref, wg0_ref, wg1_ref, wg2_ref, w0_ref, w1_ref, w2_ref, z_ref):
    x = x_ref[...]
    z = jax.nn.sigmoid(_dot(x, wg0_ref[...])) * _dot(o_ref[...], w0_ref[...])
    z = z + jax.nn.sigmoid(_dot(x, wg1_ref[...])) * _dot(u_ref[...], w1_ref[...])
    z = z + jax.nn.sigmoid(_dot(x, wg2_ref[...])) * _dot(m_ref[...], w2_ref[...])
    z_ref[...] = z.astype(BF16)


def _merge(l, x_bf, o_mla, u_conv, o_mem, w_in_p, col_gate, w_o_mla, w_conv_out, w_o_mem):
    t, d = x_bf.shape
    bm, bn = _blk(t, 512), _blk(d, 256)
    nj = d // bn
    act = lambda a: pl.BlockSpec((bm, a.shape[1]), lambda j, i: (i, 0))
    gate = lambda b: pl.BlockSpec((None, d, bn), lambda j, i, b=b: (l, 0, col_gate // bn + b * nj + j))
    wbr = lambda w: pl.BlockSpec((None, w.shape[1], bn), lambda j, i: (l, 0, j))
    return pl.pallas_call(
        _merge_kernel,
        grid=(nj, t // bm),
        in_specs=[act(x_bf), act(o_mla), act(u_conv), act(o_mem), gate(0), gate(1), gate(2),
                  wbr(w_o_mla), wbr(w_conv_out), wbr(w_o_mem)],
        out_specs=pl.BlockSpec((bm, bn), lambda j, i: (i, j)),
        out_shape=jax.ShapeDtypeStruct((t, d), BF16),
        compiler_params=_params(("parallel", "parallel")),
        name="gated_merge",
    )(x_bf, o_mla, u_conv, o_mem, w_in_p, w_in_p, w_in_p, w_o_mla, w_conv_out, w_o_mem)


def _route_rows(logits, b_router):
    scores = jax.nn.sigmoid(logits[:N_EXPERTS])
    biased = scores + b_router[:N_EXPERTS]
    sc = [scores[e:e + 1] for e in range(N_EXPERTS)]
    bg = [biased[e:e + 1] for e in range(N_EXPERTS)]
    group_score = []
    for g in range(N_GROUPS):
        v = bg[g * EXPERTS_PER_GROUP:(g + 1) * EXPERTS_PER_GROUP]
        best = None
        for a in range(EXPERTS_PER_GROUP):
            for b in range(a + 1, EXPERTS_PER_GROUP):
                pair = v[a] + v[b]
                best = pair if best is None else jnp.maximum(best, pair)
        group_score.append(best)
    best, best_g = group_score[0], jnp.zeros_like(group_score[0], dtype=jnp.int32)
    for g in range(1, N_GROUPS):
        better = group_score[g] > best
        best = jnp.where(better, group_score[g], best)
        best_g = jnp.where(better, g, best_g)
    sel, picked = [], []
    for e in range(N_EXPERTS):
        g, j = divmod(e, EXPERTS_PER_GROUP)
        rank = jnp.zeros_like(best_g)
        for jj in range(EXPERTS_PER_GROUP):
            if jj == j:
                continue
            other = bg[g * EXPERTS_PER_GROUP + jj]
            ahead = (other > bg[e]) | ((other == bg[e]) & (jj < j))
            rank = rank + ahead.astype(jnp.int32)
        s = (best_g == g) & (rank < TOP_K)
        sel.append(s)
        picked.append(jnp.where(s, sc[e], 0.0))
    denom = picked[0]
    for e in range(1, N_EXPERTS):
        denom = denom + picked[e]
    return [p / denom for p in picked], sel


def _pack_bf16_pair(lo, hi):
    lo_bits = lax.bitcast_convert_type(lo.astype(F32), jnp.uint32) >> 16
    hi_bits = lax.bitcast_convert_type(hi.astype(F32), jnp.uint32) & jnp.uint32(0xFFFF0000)
    return hi_bits | lo_bits


def _unpack_bf16_pair(words):
    lo = lax.bitcast_convert_type(words << 16, F32).astype(BF16)
    hi = lax.bitcast_convert_type(words & jnp.uint32(0xFFFF0000), F32).astype(BF16)
    return lo, hi


def _outproj_ln_kernel(z_ref, w_ref, x_ref, g_ref, b_ref, wr_ref, br_ref,
                       xo_ref, xpk_ref, route_ref, acc_ref, *, alpha, nj, bn):
    j = pl.program_id(1)
    acc_ref[j] = alpha * x_ref[...] + _dot(z_ref[...], w_ref[...])

    @pl.when(j == nj - 1)
    def _():
        d = nj * bn
        tot = jnp.sum(acc_ref[0], -1, keepdims=True)
        for jj in range(1, nj):
            tot = tot + jnp.sum(acc_ref[jj], -1, keepdims=True)
        mu = tot / d
        sq = None
        for jj in range(nj):
            dv = acc_ref[jj] - mu
            part = jnp.sum(dv * dv, -1, keepdims=True)
            sq = part if sq is None else sq + part
        rstd = lax.rsqrt(sq / d + LN_EPS)
        logits = None
        half = nj // 2
        for jj in range(half):
            pair = []
            for c in (jj, jj + half):
                lo = c * bn
                v = (acc_ref[c] - mu) * rstd * g_ref[:, lo:lo + bn] + b_ref[:, lo:lo + bn]
                xo_ref[:, lo:lo + bn] = v
                vb = v.astype(BF16)
                part = _dot_nt(wr_ref[:, lo:lo + bn], vb)
                logits = part if logits is None else logits + part
                pair.append(vb)
            xpk_ref[:, jj * bn:(jj + 1) * bn] = _pack_bf16_pair(pair[0], pair[1])
        cw, sel = _route_rows(logits, br_ref[...])
        for e in range(N_EXPERTS):
            route_ref[e:e + 1, :] = cw[e]
            route_ref[N_EXPERTS + e:N_EXPERTS + e + 1, :] = sel[e].astype(F32)


def _outproj_ln_route(l, z, w_out, x_res, ln_g, ln_b, w_router_t, b_router_col, alpha):
    t, d = z.shape
    bm, bn = _blk(t, 512), _blk(d // 2, 512)
    nj = d // bn
    fixed = lambda i, j: (0, 0)
    layer = lambda i, j: (l, 0, 0)
    return pl.pallas_call(
        functools.partial(_outproj_ln_kernel, alpha=alpha, nj=nj, bn=bn),
        grid=(t // bm, nj),
        in_specs=[
            pl.BlockSpec((bm, d), lambda i, j: (i, 0)),
            pl.BlockSpec((None, d, bn), lambda i, j: (l, 0, j)),
            pl.BlockSpec((bm, bn), lambda i, j: (i, j)),
            _resident((None, 1, d), layer),
            _resident((None, 1, d), layer),
            _resident((LANES, d), fixed),
            _resident((LANES, 1), fixed),
        ],
        out_specs=[
            pl.BlockSpec((bm, d), lambda i, j: (i, 0)),
            pl.BlockSpec((bm, d // 2), lambda i, j: (i, 0)),
            pl.BlockSpec((ROUTE_ROWS, bm), lambda i, j: (0, i)),
        ],
        out_shape=[
            jax.ShapeDtypeStruct((t, d), F32),
            jax.ShapeDtypeStruct((t, d // 2), jnp.uint32),
            jax.ShapeDtypeStruct((ROUTE_ROWS, t), F32),
        ],
        scratch_shapes=[pltpu.VMEM((nj, bm, bn), F32)],
        compiler_params=_params(("parallel", "arbitrary"), VMEM_LIMIT_OUTPROJ),
        name="outproj_ln_route",
    )(z, w_out, x_res, ln_g, ln_b, w_router_t, b_router_col)


def _row_copy(src_hbm, row, dst_ref, r, sem):
    return pltpu.make_async_copy(src_hbm.at[pl.ds(row, 1), :], dst_ref.at[pl.ds(r, 1), :], sem)


FFN_BUFFERS = 3


def _ffn_kernel(be_ref, nused_ref, tok_a, tok_b, tok_c, x_hbm, wg_ref, wu_ref, wd_ref, y_ref,
                xbuf0, xbuf1, xbuf2, sems, *, rows):
    i = pl.program_id(0)
    n_used = nused_ref[0]
    bufs = (xbuf0, xbuf1, xbuf2)
    half = wg_ref.shape[0] // 2

    def gather(tok_ref, s):
        for r in range(rows):
            _row_copy(x_hbm, tok_ref[0, 0, r], bufs[s], r, sems.at[s]).start()

    @pl.when(i == 0)
    def _():
        gather(tok_a, 0)
        gather(tok_b, 1)

    @pl.when(i >= n_used)
    def _():
        y_ref[...] = jnp.zeros(y_ref.shape, y_ref.dtype)

    for s in range(FFN_BUFFERS):
        mine = i % FFN_BUFFERS == s

        @pl.when(mine & (i < n_used + 2))
        def _():
            for r in range(rows):
                _row_copy(x_hbm, 0, bufs[s], r, sems.at[s]).wait()

        @pl.when(mine & (i < n_used))
        def _():
            gather(tok_c, (s + 2) % FFN_BUFFERS)
            x_lo, x_hi = _unpack_bf16_pair(bufs[s][...])
            g = _dot(x_lo, wg_ref[:half, :]) + _dot(x_hi, wg_ref[half:, :])
            u = _dot(x_lo, wu_ref[:half, :]) + _dot(x_hi, wu_ref[half:, :])
            h = (g * jax.nn.sigmoid(g)) * u
            y_ref[...] = _dot(h.astype(BF16), wd_ref[...])


def _expert_ffn(l, x_packed, slot_tok, block_e, n_used, w_gate, w_up, w_down, rows):
    p = slot_tok.shape[0]
    dh = x_packed.shape[1]
    d = 2 * dh
    de = w_gate.shape[3]
    nb = p // rows
    tok = slot_tok.reshape(nb, 1, rows)
    wsel = lambda i, be, nu: (l, be[jnp.minimum(i, nu[0] - 1)], 0, 0)
    tok_block = lambda f: pl.BlockSpec((1, 1, rows), lambda i, be, nu: (f(i), 0, 0), memory_space=pltpu.SMEM)
    return pl.pallas_call(
        functools.partial(_ffn_kernel, rows=rows),
        grid_spec=pltpu.PrefetchScalarGridSpec(
            num_scalar_prefetch=2,
            grid=(nb,),
            in_specs=[
                tok_block(lambda i: 0),
                tok_block(lambda i: 1),
                tok_block(lambda i: jnp.minimum(i + 2, nb - 1)),
                pl.BlockSpec(memory_space=pl.ANY),
                pl.BlockSpec((None, None, d, de), wsel),
                pl.BlockSpec((None, None, d, de), wsel),
                pl.BlockSpec((None, None, de, d), wsel),
            ],
            out_specs=pl.BlockSpec((rows, d), lambda i, be, nu: (i, 0)),
            scratch_shapes=[pltpu.VMEM((rows, dh), jnp.uint32)] * FFN_BUFFERS
            + [pltpu.SemaphoreType.DMA((FFN_BUFFERS,))],
        ),
        out_shape=jax.ShapeDtypeStruct((p, d), F32),
        compiler_params=_params(("arbitrary",)),
        name="moe_expert_ffn",
    )(block_e, n_used, tok, tok, tok, x_packed, w_gate, w_up, w_down)


def _combine_kernel(pos0_ref, pos1_ref, ys_hbm, x_ref, wts_ref, g_ref, b_ref, xo_ref, xbf_ref,
                    a0, b0, a1, b1, sems, *, alpha, rows, nb):
    i = pl.program_id(0)
    bufs = ((a0, b0), (a1, b1))

    def gather(pos_ref, s):
        for r in range(rows):
            _row_copy(ys_hbm, pos_ref[0, 0, r], bufs[s][0], r, sems.at[s]).start()
            _row_copy(ys_hbm, pos_ref[0, 1, r], bufs[s][1], r, sems.at[s]).start()

    def drain(s):
        for r in range(rows):
            _row_copy(ys_hbm, 0, bufs[s][0], r, sems.at[s]).wait()
            _row_copy(ys_hbm, 0, bufs[s][1], r, sems.at[s]).wait()

    @pl.when(i == 0)
    def _():
        gather(pos0_ref, 0)

    for s in range(2):
        @pl.when(i % 2 == s)
        def _():
            drain(s)
            gather(pos1_ref, 1 - s)
            y = wts_ref[:, 0:1] * bufs[s][0][...] + wts_ref[:, 1:2] * bufs[s][1][...]
            v = _layer_norm_rows(alpha * x_ref[...] + y, g_ref[...], b_ref[...])
            xo_ref[...] = v
            xbf_ref[...] = v.astype(BF16)

        @pl.when((i % 2 == s) & (i == nb - 1))
        def _():
            drain(1 - s)


def _combine_ln(l, ys, pos, wts, x_res, ln_g, ln_b, alpha):
    t, d = x_res.shape
    rows = _blk(t, 128)
    nb = t // rows
    layer = lambda i: (l, 0, 0)
    pos_blocks = pos.reshape(nb, rows, 2).transpose(0, 2, 1)
    return pl.pallas_call(
        functools.partial(_combine_kernel, alpha=alpha, rows=rows, nb=nb),
        grid=(nb,),
        in_specs=[
            pl.BlockSpec((1, 2, rows), lambda i: (0, 0, 0), memory_space=pltpu.SMEM),
            pl.BlockSpec((1, 2, rows), lambda i: (jnp.minimum(i + 1, nb - 1), 0, 0), memory_space=pltpu.SMEM),
            pl.BlockSpec(memory_space=pl.ANY),
            pl.BlockSpec((rows, d), lambda i: (i, 0)),
            pl.BlockSpec((rows, LANES), lambda i: (i, 0)),
            _resident((None, 1, d), layer),
            _resident((None, 1, d), layer),
        ],
        out_specs=[pl.BlockSpec((rows, d), lambda i: (i, 0)), pl.BlockSpec((rows, d), lambda i: (i, 0))],
        out_shape=[jax.ShapeDtypeStruct((t, d), F32), jax.ShapeDtypeStruct((t, d), BF16)],
        scratch_shapes=[pltpu.VMEM((rows, d), F32)] * 4 + [pltpu.SemaphoreType.DMA((2,))],
        compiler_params=_params(("arbitrary",)),
        name="moe_combine_ln",
    )(pos_blocks, pos_blocks, ys, x_res, wts, ln_g, ln_b)


def _dispatch_plan(route_t, rows):
    t = route_t.shape[1]
    cw = route_t[:N_EXPERTS].T
    sel = route_t[N_EXPERTS:].T > 0.5
    sel_i = sel.astype(jnp.int32)
    csum = jnp.cumsum(sel_i, axis=0)
    counts = csum[-1]
    padded = (counts + rows - 1) // rows * rows
    seg_end = jnp.cumsum(padded)
    dest = (seg_end - padded)[None, :] + (csum - sel_i)
    nb = t * TOP_K // rows + N_EXPERTS + FFN_BUFFERS - 1
    n_slots = nb * rows
    pos0 = jnp.min(jnp.where(sel, dest, n_slots), axis=1)
    pos1 = jnp.max(jnp.where(sel, dest, -1), axis=1)
    w0 = jnp.sum(jnp.where(sel & (dest == pos0[:, None]), cw, 0.0), axis=1)
    w1 = jnp.sum(jnp.where(sel & (dest == pos1[:, None]), cw, 0.0), axis=1)
    tok = jnp.arange(t, dtype=jnp.int32)
    slot_tok = jnp.zeros((n_slots,), jnp.int32).at[jnp.concatenate([pos0, pos1])].set(
        jnp.concatenate([tok, tok]), unique_indices=True)
    block_start = jnp.arange(nb, dtype=jnp.int32) * rows
    block_e = jnp.minimum(jnp.sum((seg_end[None, :] <= block_start[:, None]).astype(jnp.int32), axis=1),
                          N_EXPERTS - 1).astype(jnp.int32)
    n_used = (seg_end[-1] // rows).astype(jnp.int32).reshape(1)
    pos = jnp.stack([pos0, pos1], axis=1).astype(jnp.int32)
    wts = jnp.zeros((t, LANES), F32).at[:, 0].set(w0).at[:, 1].set(w1)
    return slot_tok, block_e, n_used, pos, wts


def _rope_tables(positions):
    inv = 1.0 / (ROPE_THETA ** (jnp.arange(0, QK_ROPE, 2, dtype=F32) / QK_ROPE))
    ang = positions.astype(F32).reshape(-1, 1) * inv
    cos, sin = jnp.cos(ang), jnp.sin(ang)
    zero = jnp.zeros_like(cos)
    pad = jnp.zeros((cos.shape[0], LANES - QK_ROPE), F32)
    c_tab = jnp.concatenate([cos, cos, pad], 1)
    s1_tab = jnp.concatenate([zero, sin, pad], 1)
    s2_tab = jnp.concatenate([-sin, zero, pad], 1)
    return c_tab, s1_tab, s2_tab


def kernel(x, mem, positions, w_in, q_norm_g, kv_norm_g, w_uq, w_ukv, w_o_mla, conv_w, conv_b, conv_ln_g, conv_ln_b, w_conv_out, w_mem_kv, w_o_mem, w_out, ln1_g, ln1_b, w_router, b_router, w_gate, w_up, w_down, ln2_g, ln2_b):
    batch, seq, d = x.shape
    depth = w_in.shape[0]
    t = batch * seq
    q_lora, kv_lora, conv_dim = q_norm_g.shape[1], kv_norm_g.shape[1], conv_w.shape[2]
    mem_w = MEM_HEADS * MEM_HEAD_DIM
    assert conv_dim == mem_w
    alpha = (2 * depth) ** 0.25
    moe_rows = 256

    c_tab, s1_tab, s2_tab = _rope_tables(positions)
    mem_bf = mem.reshape(-1, d).astype(BF16)
    w_router_t = jnp.pad(w_router.T, ((0, LANES - N_EXPERTS), (0, 0))).astype(BF16)
    b_router_col = jnp.pad(b_router.astype(F32), (0, LANES - N_EXPERTS)).reshape(LANES, 1)

    o_conv = q_lora + kv_lora + QK_ROPE
    wa_cols = o_conv + LANES - QK_ROPE
    w_lat = jnp.pad(w_in[:, :, :o_conv].astype(BF16), ((0, 0), (0, 0), (0, wa_cols - o_conv)))
    w_rest = _cast_cols(w_in, o_conv)
    col_a, col_g, col_m = 0, conv_dim, 2 * conv_dim
    col_gate = col_m + mem_w

    wuq = jnp.pad(w_uq.reshape(depth, q_lora, MLA_HEADS, QK_NOPE + QK_ROPE),
                  ((0, 0), (0, 0), (0, 0), (0, Q_HEAD_PAD - QK_NOPE - QK_ROPE))
                  ).reshape(depth, q_lora, -1).astype(BF16)
    wukv, w_o_mla_b, w_conv_out_b = w_ukv.astype(BF16), w_o_mla.astype(BF16), w_conv_out.astype(BF16)
    w_mem_kv_b, w_o_mem_b, w_out_b = w_mem_kv.astype(BF16), w_o_mem.astype(BF16), w_out.astype(BF16)
    w_gate_b, w_up_b, w_down_b = w_gate.astype(BF16), w_up.astype(BF16), w_down.astype(BF16)
    conv_w_p = jnp.pad(conv_w, ((0, 0), (0, CONV_HALO - CONV_WIDTH), (0, 0)))
    vec = lambda a: a[:, None, :]
    qg, kvg, conv_b3, conv_g3, conv_beta3 = vec(q_norm_g), vec(kv_norm_g), vec(conv_b), vec(conv_ln_g), vec(conv_ln_b)
    ln1_g3, ln1_b3, ln2_g3, ln2_b3 = vec(ln1_g), vec(ln1_b), vec(ln2_g), vec(ln2_b)

    xf = x.reshape(t, d)
    xb = xf.astype(BF16)
    for l in range(depth):
        q, kv, kr = _mla_proj(l, xb, w_lat, wa_cols, wuq, wukv, qg, kvg, c_tab, s1_tab, s2_tab)
        o_mla = _attention(q.reshape(batch, seq, -1), kv.reshape(batch, seq, -1),
                           kr.reshape(batch, seq, LANES), batch, seq).reshape(t, -1)

        u0, mq = _xproj(l, xb, w_rest, col_a, col_g, col_m, conv_dim)
        u_conv = _conv_module(l, u0.reshape(batch, seq, conv_dim), conv_w_p, conv_b3, conv_g3, conv_beta3,
                              batch, seq).reshape(t, -1)

        kvm = _matmul(l, mem_bf, w_mem_kv_b, BF16)
        o_mem = _mem_attention(mq.reshape(batch, seq, mem_w), kvm.reshape(batch, -1, 2 * mem_w),
                               batch, seq).reshape(t, -1)

        z = _merge(l, xb, o_mla, u_conv, o_mem, w_rest, col_gate, w_o_mla_b, w_conv_out_b, w_o_mem_b)
        x1, x1_packed, route_t = _outproj_ln_route(l, z, w_out_b, xf, ln1_g3, ln1_b3, w_router_t, b_router_col,
                                                   alpha)

        slot_tok, block_e, n_used, pos, wts = _dispatch_plan(route_t, moe_rows)
        ys = _expert_ffn(l, x1_packed, slot_tok, block_e, n_used, w_gate_b, w_up_b, w_down_b, moe_rows)
        xf, xb = _combine_ln(l, ys, pos, wts, x1, ln2_g3, ln2_b3, alpha)
    return xf.reshape(batch, seq, d)
```

```python
import functools

import jax
import jax.numpy as jnp
from jax import lax
from jax.experimental import pallas as pl
from jax.experimental.pallas import tpu as pltpu

F32 = jnp.float32
BF16 = jnp.bfloat16

CHUNK = 64
MLA_HEADS = 16
QK_NOPE = 128
QK_ROPE = 64
V_DIM = 128
ROPE_THETA = 10000.0
MLA_SCALE = (QK_NOPE + QK_ROPE) ** -0.5
CONV_WIDTH = 31
MEM_HEADS = 4
MEM_HEAD_DIM = 256
MEM_SCALE = MEM_HEAD_DIM ** -0.5
N_EXPERTS = 16
N_GROUPS = 4
EXPERTS_PER_GROUP = N_EXPERTS // N_GROUPS
TOP_K = 2
LN_EPS = 1e-5
RMS_EPS = 1e-6

LANES = 128
SUBLANES = 8
Q_HEAD_PAD = 2 * LANES
CONV_HALO = 32
ROUTE_ROWS = 2 * N_EXPERTS
VMEM_LIMIT = 56 * 1024 * 1024
VMEM_LIMIT_OUTPROJ = 60 * 1024 * 1024
NEG_BIG = -1e30
LOG2_E = 1.4426950408889634


def _params(dims, vmem_limit=VMEM_LIMIT):
    return pltpu.CompilerParams(dimension_semantics=dims, vmem_limit_bytes=vmem_limit)


def _blk(dim, pref):
    b = min(dim, pref)
    assert dim % b == 0, (dim, pref)
    return b


def _resident(shape, index_map):
    return pl.BlockSpec(shape, index_map, pipeline_mode=pl.Buffered(1))


def _dot(a, b):
    return jnp.dot(a, b, preferred_element_type=F32)


def _dot_nt(a, b):
    return lax.dot_general(a, b, (((1,), (1,)), ((), ())), preferred_element_type=F32)


def _layer_norm_rows(v, g, b):
    mu = jnp.mean(v, -1, keepdims=True)
    d = v - mu
    var = jnp.mean(d * d, -1, keepdims=True)
    return d * lax.rsqrt(var + LN_EPS) * g + b


CAST_PIECE = QK_ROPE
CAST_PIECES = 4


def _cast_t_kernel(*refs):
    o_ref = refs[-1]
    w = jnp.concatenate([r[...] for r in refs[:-1]], axis=0)
    o_ref[...] = w.T.astype(BF16)


def _cast_transposed(w_t, start):
    depth, n, k = w_t.shape
    width = n - start
    bn = CAST_PIECE * CAST_PIECES
    assert start % CAST_PIECE == 0 and width % bn == 0
    first = start // CAST_PIECE
    piece = lambda p: pl.BlockSpec((None, CAST_PIECE, k), lambda l, j, p=p: (l, first + CAST_PIECES * j + p, 0))
    return pl.pallas_call(
        _cast_t_kernel,
        grid=(depth, width // bn),
        in_specs=[piece(p) for p in range(CAST_PIECES)],
        out_specs=pl.BlockSpec((None, k, bn), lambda l, j: (l, 0, j)),
        out_shape=jax.ShapeDtypeStruct((depth, k, width), BF16),
        compiler_params=_params(("parallel", "parallel")),
        name="cast_transposed",
    )(*([w_t] * CAST_PIECES))


def _rope_lanes(t, c, s1, s2):
    return t * c + pltpu.roll(t, 32, 1) * s1 + pltpu.roll(t, 96, 1) * s2


def _mla_proj_kernel(x_ref, wa_ref, wuq_ref, wukv_ref, qg_ref, kvg_ref, c_ref, s1_ref, s2_ref,
                     q_ref, kv_ref, kr_ref, *, q_lora, kv_lora):
    h = _dot(x_ref[...], wa_ref[...])
    cq = h[:, :q_lora]
    ckv = h[:, q_lora:q_lora + kv_lora]
    kr = h[:, q_lora + kv_lora:q_lora + kv_lora + LANES]

    def rms(t, g):
        return t * lax.rsqrt(jnp.mean(t * t, -1, keepdims=True) + RMS_EPS) * g

    c, s1, s2 = c_ref[...], s1_ref[...], s2_ref[...]
    kr_ref[...] = _rope_lanes(kr, c, s1, s2).astype(BF16)
    kv_ref[...] = _dot(rms(ckv, kvg_ref[...]).astype(BF16), wukv_ref[...]).astype(BF16)
    q = _dot(rms(cq, qg_ref[...]).astype(BF16), wuq_ref[...])
    for hh in range(MLA_HEADS):
        lo = hh * Q_HEAD_PAD
        q_ref[:, lo:lo + LANES] = q[:, lo:lo + LANES].astype(BF16)
        q_ref[:, lo + LANES:lo + Q_HEAD_PAD] = _rope_lanes(
            q[:, lo + LANES:lo + Q_HEAD_PAD], c, s1, s2).astype(BF16)


def _mla_proj(l, x_bf, w_in_p, wa_cols, wuq, wukv, qg, kvg, c_tab, s1_tab, s2_tab):
    t, d = x_bf.shape
    q_lora, kv_lora = qg.shape[2], kvg.shape[2]
    nq, nkv = wuq.shape[2], wukv.shape[2]
    bm = _blk(t, 256)
    row = lambda i: (i, 0)
    layer = lambda i: (l, 0, 0)
    return pl.pallas_call(
        functools.partial(_mla_proj_kernel, q_lora=q_lora, kv_lora=kv_lora),
        grid=(t // bm,),
        in_specs=[
            pl.BlockSpec((bm, d), row),
            _resident((None, d, wa_cols), layer),
            _resident((None, q_lora, nq), layer),
            _resident((None, kv_lora, nkv), layer),
            _resident((None, 1, q_lora), layer),
            _resident((None, 1, kv_lora), layer),
            pl.BlockSpec((bm, LANES), row),
            pl.BlockSpec((bm, LANES), row),
            pl.BlockSpec((bm, LANES), row),
        ],
        out_specs=[
            pl.BlockSpec((bm, nq), row),
            pl.BlockSpec((bm, nkv), row),
            pl.BlockSpec((bm, LANES), row),
        ],
        out_shape=[
            jax.ShapeDtypeStruct((t, nq), BF16),
            jax.ShapeDtypeStruct((t, nkv), BF16),
            jax.ShapeDtypeStruct((t, LANES), BF16),
        ],
        compiler_params=_params(("parallel",)),
        name="mla_proj",
    )(x_bf, w_in_p, wuq, wukv, qg, kvg, c_tab, s1_tab, s2_tab)


def _attn_kernel(q_ref, kv_ref, kr_ref, o_ref, kcat_ref, vaug_ref, *, seq, tq):
    kcat_ref[:, :LANES] = kv_ref[0, :, :LANES]
    kcat_ref[:, LANES:] = kr_ref[0]
    vaug_ref[:, :LANES] = kv_ref[0, :, LANES:]
    ones_col = lax.broadcasted_iota(jnp.int32, (seq, LANES), 1) == 0
    vaug_ref[:, LANES:] = jnp.where(ones_col, 1.0, 0.0).astype(BF16)
    rq = lax.broadcasted_iota(jnp.int32, (tq, tq), 0) // CHUNK
    ck = lax.broadcasted_iota(jnp.int32, (tq, tq), 1) // CHUNK
    diag_ok = ck <= rq
    expo = MLA_SCALE * LOG2_E
    for i in range(seq // tq):
        lo = i * tq
        q = q_ref[0, lo:lo + tq, :]
        s_d = jnp.where(diag_ok, _dot_nt(q, kcat_ref[lo:lo + tq, :]), NEG_BIG)
        m = jnp.max(s_d, -1, keepdims=True)
        if i > 0:
            s_f = _dot_nt(q, kcat_ref[:lo, :])
            m = jnp.maximum(m, jnp.max(s_f, -1, keepdims=True))
        o = _dot(jnp.exp2((s_d - m) * expo).astype(BF16), vaug_ref[lo:lo + tq, :])
        if i > 0:
            o = o + _dot(jnp.exp2((s_f - m) * expo).astype(BF16), vaug_ref[:lo, :])
        o_ref[0, lo:lo + tq, :] = (o[:, :LANES] / o[:, LANES:LANES + 1]).astype(BF16)


def _attention(q, kv, kr, batch, seq):
    tq = _blk(seq, 256)
    return pl.pallas_call(
        functools.partial(_attn_kernel, seq=seq, tq=tq),
        grid=(batch, MLA_HEADS),
        in_specs=[
            pl.BlockSpec((1, seq, Q_HEAD_PAD), lambda b, h: (b, 0, h)),
            pl.BlockSpec((1, seq, QK_NOPE + V_DIM), lambda b, h: (b, 0, h)),
            pl.BlockSpec((1, seq, LANES), lambda b, h: (b, 0, 0)),
        ],
        out_specs=pl.BlockSpec((1, seq, V_DIM), lambda b, h: (b, 0, h)),
        out_shape=jax.ShapeDtypeStruct((batch, seq, MLA_HEADS * V_DIM), BF16),
        scratch_shapes=[pltpu.VMEM((seq, Q_HEAD_PAD), BF16), pltpu.VMEM((seq, V_DIM + LANES), BF16)],
        compiler_params=_params(("parallel", "parallel")),
        name="mla_attention",
    )(q, kv, kr)


def _xproj_kernel(x_ref, wa_ref, wg_ref, wm_ref, u_ref, mq_ref):
    x = x_ref[...]
    u_ref[...] = _dot(x, wa_ref[...]) * jax.nn.sigmoid(_dot(x, wg_ref[...]))
    mq_ref[...] = _dot(x, wm_ref[...]).astype(BF16)


def _xproj(l, x_bf, w_in_p, col_a, col_g, col_m, n):
    t, d = x_bf.shape
    bm, bn = _blk(t, 512), _blk(n, 512)
    wspec = lambda col: pl.BlockSpec((None, d, bn), lambda j, i: (l, 0, col // bn + j))
    ospec = pl.BlockSpec((bm, bn), lambda j, i: (i, j))
    return pl.pallas_call(
        _xproj_kernel,
        grid=(n // bn, t // bm),
        in_specs=[pl.BlockSpec((bm, d), lambda j, i: (i, 0)), wspec(col_a), wspec(col_g), wspec(col_m)],
        out_specs=[ospec, ospec],
        out_shape=[jax.ShapeDtypeStruct((t, n), F32), jax.ShapeDtypeStruct((t, n), BF16)],
        compiler_params=_params(("parallel", "parallel")),
        name="xproj_glu_memq",
    )(x_bf, w_in_p, w_in_p, w_in_p)


def _mm_kernel(a_ref, w_ref, o_ref):
    o_ref[...] = _dot(a_ref[...], w_ref[...]).astype(o_ref.dtype)


def _matmul_all_layers(a, w, out_dtype):
    m, k = a.shape
    depth, _, n = w.shape
    bm, bn = _blk(m, 512), _blk(n, 512)
    return pl.pallas_call(
        _mm_kernel,
        grid=(depth, n // bn, m // bm),
        in_specs=[pl.BlockSpec((bm, k), lambda l, j, i: (i, 0)),
                  pl.BlockSpec((None, k, bn), lambda l, j, i: (l, 0, j))],
        out_specs=pl.BlockSpec((None, bm, bn), lambda l, j, i: (l, i, j)),
        out_shape=jax.ShapeDtypeStruct((depth, m, n), out_dtype),
        compiler_params=_params(("parallel", "parallel", "parallel")),
        name="matmul_all_layers",
    )(a, w)


def _conv_kernel(cur_ref, prev_ref, w_ref, b_ref, g_ref, beta_ref, o_ref, win_ref, *, ts, rt):
    i = pl.program_id(1)
    win_ref[0:CONV_HALO, :] = jnp.where(i > 0, prev_ref[0, ts - CONV_HALO:ts, :], 0.0)
    win_ref[CONV_HALO:CONV_HALO + ts, :] = cur_ref[0]
    win_ref[CONV_HALO + ts:, :] = jnp.zeros((SUBLANES, win_ref.shape[1]), F32)
    first_tap = CONV_HALO - (CONV_WIDTH - 1)
    for r in range(ts // rt):
        r0 = r * rt
        acc = None
        for phase in range(SUBLANES):
            ys = None
            for k in range(CONV_WIDTH):
                off = first_tap + k
                if off % SUBLANES != phase:
                    continue
                base = r0 + off - phase
                term = win_ref[base:base + rt + SUBLANES, :] * w_ref[k:k + 1, :]
                ys = term if ys is None else ys + term
            part = ys[phase:phase + rt, :]
            acc = part if acc is None else acc + part
        v = _layer_norm_rows(acc + b_ref[...], g_ref[...], beta_ref[...])
        o_ref[0, r0:r0 + rt, :] = (v * jax.nn.sigmoid(v)).astype(BF16)


def _conv_module(l, u0, conv_w, conv_b, ln_g, ln_b, batch, seq):
    c = u0.shape[-1]
    ts = _blk(seq, 256)
    assert ts >= CONV_HALO
    rt = _blk(ts, 32)
    vec = lambda b, i: (l, 0, 0)
    return pl.pallas_call(
        functools.partial(_conv_kernel, ts=ts, rt=rt),
        grid=(batch, seq // ts),
        in_specs=[
            pl.BlockSpec((1, ts, c), lambda b, i: (b, i, 0)),
            pl.BlockSpec((1, ts, c), lambda b, i: (b, jnp.maximum(i - 1, 0), 0)),
            pl.BlockSpec((None, CONV_HALO, c), vec),
            pl.BlockSpec((None, 1, c), vec),
            pl.BlockSpec((None, 1, c), vec),
            pl.BlockSpec((None, 1, c), vec),
        ],
        out_specs=pl.BlockSpec((1, ts, c), lambda b, i: (b, i, 0)),
        out_shape=jax.ShapeDtypeStruct((batch, seq, c), BF16),
        scratch_shapes=[pltpu.VMEM((CONV_HALO + ts + SUBLANES, c), F32)],
        compiler_params=_params(("parallel", "arbitrary")),
        name="conv_module",
    )(u0, u0, conv_w, conv_b, ln_g, ln_b)


def _mem_attn_kernel(q_ref, kv_ref, o_ref):
    width = MEM_HEADS * MEM_HEAD_DIM
    for h in range(MEM_HEADS):
        lo = h * MEM_HEAD_DIM
        s = _dot_nt(q_ref[0, :, lo:lo + MEM_HEAD_DIM], kv_ref[0, :, lo:lo + MEM_HEAD_DIM]) * MEM_SCALE
        p = jnp.exp(s - jnp.max(s, -1, keepdims=True))
        l = jnp.sum(p, -1, keepdims=True)
        o = _dot(p.astype(BF16), kv_ref[0, :, width + lo:width + lo + MEM_HEAD_DIM])
        o_ref[0, :, lo:lo + MEM_HEAD_DIM] = (o / l).astype(BF16)


def _mem_attention(l, mq, kvm, batch, seq):
    width = MEM_HEADS * MEM_HEAD_DIM
    m = kvm.shape[1]
    ts = _blk(seq, 512)
    return pl.pallas_call(
        _mem_attn_kernel,
        grid=(batch, seq // ts),
        in_specs=[
            pl.BlockSpec((1, ts, width), lambda b, i: (b, i, 0)),
            pl.BlockSpec((1, m, 2 * width), lambda b, i: (l * batch + b, 0, 0)),
        ],
        out_specs=pl.BlockSpec((1, ts, width), lambda b, i: (b, i, 0)),
        out_shape=jax.ShapeDtypeStruct((batch, seq, width), BF16),
        compiler_params=_params(("parallel", "parallel")),
        name="mem_attention",
    )(mq, kvm)


def _merge_kernel(x_ref, o_ref, u_ref, m_ref, wg0_ref, wg1_ref, wg2_ref, w0_ref, w1_ref, w2_ref, z_ref):
    x = x_ref[...]
    z = jax.nn.sigmoid(_dot(x, wg0_ref[...])) * _dot(o_ref[...], w0_ref[...])
    z = z + jax.nn.sigmoid(_dot(x, wg1_ref[...])) * _dot(u_ref[...], w1_ref[...])
    z = z + jax.nn.sigmoid(_dot(x, wg2_ref[...])) * _dot(m_ref[...], w2_ref[...])
    z_ref[...] = z.astype(BF16)


def _merge(l, x_bf, o_mla, u_conv, o_mem, w_in_p, col_gate, w_o_mla, w_conv_out, w_o_mem):
    t, d = x_bf.shape
    bm, bn = _blk(t, 512), _blk(d, 256)
    nj = d // bn
    act = lambda a: pl.BlockSpec((bm, a.shape[1]), lambda j, i: (i, 0))
    gate = lambda b: pl.BlockSpec((None, d, bn), lambda j, i, b=b: (l, 0, col_gate // bn + b * nj + j))
    wbr = lambda w: pl.BlockSpec((None, w.shape[1], bn), lambda j, i: (l, 0, j))
    return pl.pallas_call(
        _merge_kernel,
        grid=(nj, t // bm),
        in_specs=[act(x_bf), act(o_mla), act(u_conv), act(o_mem), gate(0), gate(1), gate(2),
                  wbr(w_o_mla), wbr(w_conv_out), wbr(w_o_mem)],
        out_specs=pl.BlockSpec((bm, bn), lambda j, i: (i, j)),
        out_shape=jax.ShapeDtypeStruct((t, d), BF16),
        compiler_params=_params(("parallel", "parallel")),
        name="gated_merge",
    )(x_bf, o_mla, u_conv, o_mem, w_in_p, w_in_p, w_in_p, w_o_mla, w_conv_out, w_o_mem)


def _route_rows(logits, b_router):
    scores = jax.nn.sigmoid(logits[:N_EXPERTS])
    biased = scores + b_router[:N_EXPERTS]
    sc = [scores[e:e + 1] for e in range(N_EXPERTS)]
    bg = [biased[e:e + 1] for e in range(N_EXPERTS)]
    group_score = []
    for g in range(N_GROUPS):
        v = bg[g * EXPERTS_PER_GROUP:(g + 1) * EXPERTS_PER_GROUP]
        best = None
        for a in range(EXPERTS_PER_GROUP):
            for b in range(a + 1, EXPERTS_PER_GROUP):
                pair = v[a] + v[b]
                best = pair if best is None else jnp.maximum(best, pair)
        group_score.append(best)
    best, best_g = group_score[0], jnp.zeros_like(group_score[0], dtype=jnp.int32)
    for g in range(1, N_GROUPS):
        better = group_score[g] > best
        best = jnp.where(better, group_score[g], best)
        best_g = jnp.where(better, g, best_g)
    sel, picked = [], []
    for e in range(N_EXPERTS):
        g, j = divmod(e, EXPERTS_PER_GROUP)
        rank = jnp.zeros_like(best_g)
        for jj in range(EXPERTS_PER_GROUP):
            if jj == j:
                continue
            other = bg[g * EXPERTS_PER_GROUP + jj]
            ahead = (other > bg[e]) | ((other == bg[e]) & (jj < j))
            rank = rank + ahead.astype(jnp.int32)
        s = (best_g == g) & (rank < TOP_K)
        sel.append(s)
        picked.append(jnp.where(s, sc[e], 0.0))
    denom = picked[0]
    for e in range(1, N_EXPERTS):
        denom = denom + picked[e]
    return [p / denom for p in picked], sel


def _pack_bf16_pair(lo, hi):
    lo_bits = lax.bitcast_convert_type(lo.astype(F32), jnp.uint32) >> 16
    hi_bits = lax.bitcast_convert_type(hi.astype(F32), jnp.uint32) & jnp.uint32(0xFFFF0000)
    return hi_bits | lo_bits


def _unpack_bf16_pair(words):
    lo = lax.bitcast_convert_type(words << 16, F32).astype(BF16)
    hi = lax.bitcast_convert_type(words & jnp.uint32(0xFFFF0000), F32).astype(BF16)
    return lo, hi


def _outproj_ln_kernel(z_ref, w_ref, x_ref, g_ref, b_ref, wr_ref, br_ref,
                       xo_ref, xpk_ref, route_ref, acc_ref, *, alpha, nj, bn):
    j = pl.program_id(1)
    acc_ref[j] = alpha * x_ref[...] + _dot(z_ref[...], w_ref[...])

    @pl.when(j == nj - 1)
    def _():
        d = nj * bn
        tot = jnp.sum(acc_ref[0], -1, keepdims=True)
        for jj in range(1, nj):
            tot = tot + jnp.sum(acc_ref[jj], -1, keepdims=True)
        mu = tot / d
        sq = None
        for jj in range(nj):
            dv = acc_ref[jj] - mu
            part = jnp.sum(dv * dv, -1, keepdims=True)
            sq = part if sq is None else sq + part
        rstd = lax.rsqrt(sq / d + LN_EPS)
        logits = None
        half = nj // 2
        for jj in range(half):
            pair = []
            for c in (jj, jj + half):
                lo = c * bn
                v = (acc_ref[c] - mu) * rstd * g_ref[:, lo:lo + bn] + b_ref[:, lo:lo + bn]
                xo_ref[:, lo:lo + bn] = v
                vb = v.astype(BF16)
                part = _dot_nt(wr_ref[:, lo:lo + bn], vb)
                logits = part if logits is None else logits + part
                pair.append(vb)
            xpk_ref[:, jj * bn:(jj + 1) * bn] = _pack_bf16_pair(pair[0], pair[1])
        cw, sel = _route_rows(logits, br_ref[...])
        for e in range(N_EXPERTS):
            route_ref[e:e + 1, :] = cw[e]
            route_ref[N_EXPERTS + e:N_EXPERTS + e + 1, :] = sel[e].astype(F32)


def _outproj_ln_route(l, z, w_out, x_res, ln_g, ln_b, w_router_t, b_router_col, alpha):
    t, d = z.shape
    bm, bn = _blk(t, 512), _blk(d // 2, 512)
    nj = d // bn
    fixed = lambda i, j: (0, 0)
    layer = lambda i, j: (l, 0, 0)
    return pl.pallas_call(
        functools.partial(_outproj_ln_kernel, alpha=alpha, nj=nj, bn=bn),
        grid=(t // bm, nj),
        in_specs=[
            pl.BlockSpec((bm, d), lambda i, j: (i, 0)),
            pl.BlockSpec((None, d, bn), lambda i, j: (l, 0, j)),
            pl.BlockSpec((bm, bn), lambda i, j: (i, j)),
            _resident((None, 1, d), layer),
            _resident((None, 1, d), layer),
            _resident((LANES, d), fixed),
            _resident((LANES, 1), fixed),
        ],
        out_specs=[
            pl.BlockSpec((bm, d), lambda i, j: (i, 0)),
            pl.BlockSpec((bm, d // 2), lambda i, j: (i, 0)),
            pl.BlockSpec((ROUTE_ROWS, bm), lambda i, j: (0, i)),
        ],
        out_shape=[
            jax.ShapeDtypeStruct((t, d), F32),
            jax.ShapeDtypeStruct((t, d // 2), jnp.uint32),
            jax.ShapeDtypeStruct((ROUTE_ROWS, t), F32),
        ],
        scratch_shapes=[pltpu.VMEM((nj, bm, bn), F32)],
        compiler_params=_params(("parallel", "arbitrary"), VMEM_LIMIT_OUTPROJ),
        name="outproj_ln_route",
    )(z, w_out, x_res, ln_g, ln_b, w_router_t, b_router_col)


def _row_copy(src_hbm, row, dst_ref, r, sem):
    return pltpu.make_async_copy(src_hbm.at[pl.ds(row, 1), :], dst_ref.at[pl.ds(r, 1), :], sem)


FFN_BUFFERS = 3


def _ffn_kernel(be_ref, nused_ref, tok_a, tok_b, tok_c, x_hbm, wg_ref, wu_ref, wd_ref, y_ref,
                xbuf0, xbuf1, xbuf2, sems, *, rows):
    i = pl.program_id(0)
    n_used = nused_ref[0]
    bufs = (xbuf0, xbuf1, xbuf2)
    half = wg_ref.shape[0] // 2

    def gather(tok_ref, s):
        for r in range(rows):
            _row_copy(x_hbm, tok_ref[0, 0, r], bufs[s], r, sems.at[s]).start()

    @pl.when(i == 0)
    def _():
        gather(tok_a, 0)
        gather(tok_b, 1)

    @pl.when(i >= n_used)
    def _():
        y_ref[...] = jnp.zeros(y_ref.shape, y_ref.dtype)

    for s in range(FFN_BUFFERS):
        mine = i % FFN_BUFFERS == s

        @pl.when(mine & (i < n_used + 2))
        def _():
            for r in range(rows):
                _row_copy(x_hbm, 0, bufs[s], r, sems.at[s]).wait()

        @pl.when(mine & (i < n_used))
        def _():
            gather(tok_c, (s + 2) % FFN_BUFFERS)
            x_lo, x_hi = _unpack_bf16_pair(bufs[s][...])
            g = _dot(x_lo, wg_ref[:half, :]) + _dot(x_hi, wg_ref[half:, :])
            u = _dot(x_lo, wu_ref[:half, :]) + _dot(x_hi, wu_ref[half:, :])
            h = (g * jax.nn.sigmoid(g)) * u
            y_ref[...] = _dot(h.astype(BF16), wd_ref[...])


def _expert_ffn(l, x_packed, slot_tok, block_e, n_used, w_gate, w_up, w_down, rows):
    p = slot_tok.shape[0]
    dh = x_packed.shape[1]
    d = 2 * dh
    de = w_gate.shape[3]
    nb = p // rows
    tok = slot_tok.reshape(nb, 1, rows)
    wsel = lambda i, be, nu: (l, be[jnp.minimum(i, nu[0] - 1)], 0, 0)
    tok_block = lambda f: pl.BlockSpec((1, 1, rows), lambda i, be, nu: (f(i), 0, 0), memory_space=pltpu.SMEM)
    return pl.pallas_call(
        functools.partial(_ffn_kernel, rows=rows),
        grid_spec=pltpu.PrefetchScalarGridSpec(
            num_scalar_prefetch=2,
            grid=(nb,),
            in_specs=[
                tok_block(lambda i: 0),
                tok_block(lambda i: 1),
                tok_block(lambda i: jnp.minimum(i + 2, nb - 1)),
                pl.BlockSpec(memory_space=pl.ANY),
                pl.BlockSpec((None, None, d, de), wsel),
                pl.BlockSpec((None, None, d, de), wsel),
                pl.BlockSpec((None, None, de, d), wsel),
            ],
            out_specs=pl.BlockSpec((rows, d), lambda i, be, nu: (i, 0)),
            scratch_shapes=[pltpu.VMEM((rows, dh), jnp.uint32)] * FFN_BUFFERS
            + [pltpu.SemaphoreType.DMA((FFN_BUFFERS,))],
        ),
        out_shape=jax.ShapeDtypeStruct((p, d), F32),
        compiler_params=_params(("arbitrary",)),
        name="moe_expert_ffn",
    )(block_e, n_used, tok, tok, tok, x_packed, w_gate, w_up, w_down)


def _combine_kernel(pos0_ref, pos1_ref, ys_hbm, x_ref, wts_ref, g_ref, b_ref, xo_ref, xbf_ref,
                    a0, b0, a1, b1, sems, *, alpha, rows, nb):
    i = pl.program_id(0)
    bufs = ((a0, b0), (a1, b1))

    def gather(pos_ref, s):
        for r in range(rows):
            _row_copy(ys_hbm, pos_ref[0, 0, r], bufs[s][0], r, sems.at[s]).start()
            _row_copy(ys_hbm, pos_ref[0, 1, r], bufs[s][1], r, sems.at[s]).start()

    def drain(s):
        for r in range(rows):
            _row_copy(ys_hbm, 0, bufs[s][0], r, sems.at[s]).wait()
            _row_copy(ys_hbm, 0, bufs[s][1], r, sems.at[s]).wait()

    @pl.when(i == 0)
    def _():
        gather(pos0_ref, 0)

    for s in range(2):
        @pl.when(i % 2 == s)
        def _():
            drain(s)
            gather(pos1_ref, 1 - s)
            y = wts_ref[:, 0:1] * bufs[s][0][...] + wts_ref[:, 1:2] * bufs[s][1][...]
            v = _layer_norm_rows(alpha * x_ref[...] + y, g_ref[...], b_ref[...])
            xo_ref[...] = v
            xbf_ref[...] = v.astype(BF16)

        @pl.when((i % 2 == s) & (i == nb - 1))
        def _():
            drain(1 - s)


def _combine_ln(l, ys, pos, wts, x_res, ln_g, ln_b, alpha):
    t, d = x_res.shape
    rows = _blk(t, 128)
    nb = t // rows
    layer = lambda i: (l, 0, 0)
    pos_blocks = pos.reshape(nb, rows, 2).transpose(0, 2, 1)
    return pl.pallas_call(
        functools.partial(_combine_kernel, alpha=alpha, rows=rows, nb=nb),
        grid=(nb,),
        in_specs=[
            pl.BlockSpec((1, 2, rows), lambda i: (0, 0, 0), memory_space=pltpu.SMEM),
            pl.BlockSpec((1, 2, rows), lambda i: (jnp.minimum(i + 1, nb - 1), 0, 0), memory_space=pltpu.SMEM),
            pl.BlockSpec(memory_space=pl.ANY),
            pl.BlockSpec((rows, d), lambda i: (i, 0)),
            pl.BlockSpec((rows, LANES), lambda i: (i, 0)),
            _resident((None, 1, d), layer),
            _resident((None, 1, d), layer),
        ],
        out_specs=[pl.BlockSpec((rows, d), lambda i: (i, 0)), pl.BlockSpec((rows, d), lambda i: (i, 0))],
        out_shape=[jax.ShapeDtypeStruct((t, d), F32), jax.ShapeDtypeStruct((t, d), BF16)],
        scratch_shapes=[pltpu.VMEM((rows, d), F32)] * 4 + [pltpu.SemaphoreType.DMA((2,))],
        compiler_params=_params(("arbitrary",)),
        name="moe_combine_ln",
    )(pos_blocks, pos_blocks, ys, x_res, wts, ln_g, ln_b)


def _dispatch_plan(route_t, rows):
    t = route_t.shape[1]
    cw = route_t[:N_EXPERTS].T
    sel = route_t[N_EXPERTS:].T > 0.5
    sel_i = sel.astype(jnp.int32)
    csum = jnp.cumsum(sel_i, axis=0)
    counts = csum[-1]
    padded = (counts + rows - 1) // rows * rows
    seg_end = jnp.cumsum(padded)
    dest = (seg_end - padded)[None, :] + (csum - sel_i)
    nb = t * TOP_K // rows + N_EXPERTS + FFN_BUFFERS - 1
    n_slots = nb * rows
    pos0 = jnp.min(jnp.where(sel, dest, n_slots), axis=1)
    pos1 = jnp.max(jnp.where(sel, dest, -1), axis=1)
    w0 = jnp.sum(jnp.where(sel & (dest == pos0[:, None]), cw, 0.0), axis=1)
    w1 = jnp.sum(jnp.where(sel & (dest == pos1[:, None]), cw, 0.0), axis=1)
    tok = jnp.arange(t, dtype=jnp.int32)
    slot_tok = jnp.zeros((n_slots,), jnp.int32).at[jnp.concatenate([pos0, pos1])].set(
        jnp.concatenate([tok, tok]), unique_indices=True)
    block_start = jnp.arange(nb, dtype=jnp.int32) * rows
    block_e = jnp.minimum(jnp.sum((seg_end[None, :] <= block_start[:, None]).astype(jnp.int32), axis=1),
                          N_EXPERTS - 1).astype(jnp.int32)
    n_used = (seg_end[-1] // rows).astype(jnp.int32).reshape(1)
    pos = jnp.stack([pos0, pos1], axis=1).astype(jnp.int32)
    wts = jnp.zeros((t, LANES), F32).at[:, 0].set(w0).at[:, 1].set(w1)
    return slot_tok, block_e, n_used, pos, wts


def _rope_tables(positions):
    inv = 1.0 / (ROPE_THETA ** (jnp.arange(0, QK_ROPE, 2, dtype=F32) / QK_ROPE))
    ang = positions.astype(F32).reshape(-1, 1) * inv
    cos, sin = jnp.cos(ang), jnp.sin(ang)
    zero = jnp.zeros_like(cos)
    pad = jnp.zeros((cos.shape[0], LANES - QK_ROPE), F32)
    c_tab = jnp.concatenate([cos, cos, pad], 1)
    s1_tab = jnp.concatenate([zero, sin, pad], 1)
    s2_tab = jnp.concatenate([-sin, zero, pad], 1)
    return c_tab, s1_tab, s2_tab


def kernel(x, mem, positions, w_in, q_norm_g, kv_norm_g, w_uq, w_ukv, w_o_mla, conv_w, conv_b, conv_ln_g, conv_ln_b, w_conv_out, w_mem_kv, w_o_mem, w_out, ln1_g, ln1_b, w_router, b_router, w_gate, w_up, w_down, ln2_g, ln2_b):
    batch, seq, d = x.shape
    depth = w_in.shape[0]
    t = batch * seq
    q_lora, kv_lora, conv_dim = q_norm_g.shape[1], kv_norm_g.shape[1], conv_w.shape[2]
    mem_w = MEM_HEADS * MEM_HEAD_DIM
    assert conv_dim == mem_w
    alpha = (2 * depth) ** 0.25
    moe_rows = 256

    c_tab, s1_tab, s2_tab = _rope_tables(positions)
    mem_bf = mem.reshape(-1, d).astype(BF16)
    w_router_t = jnp.pad(w_router.T, ((0, LANES - N_EXPERTS), (0, 0))).astype(BF16)
    b_router_col = jnp.pad(b_router.astype(F32), (0, LANES - N_EXPERTS)).reshape(LANES, 1)

    o_conv = q_lora + kv_lora + QK_ROPE
    wa_cols = o_conv + LANES - QK_ROPE
    w_in_t = jnp.swapaxes(w_in, 1, 2)
    w_lat = jnp.pad(jnp.swapaxes(w_in_t[:, :o_conv, :], 1, 2).astype(BF16), ((0, 0), (0, 0), (0, wa_cols - o_conv)))
    w_rest = _cast_transposed(w_in_t, o_conv)
    col_a, col_g, col_m = 0, conv_dim, 2 * conv_dim
    col_gate = col_m + mem_w

    wuq = jnp.pad(w_uq.reshape(depth, q_lora, MLA_HEADS, QK_NOPE + QK_ROPE),
                  ((0, 0), (0, 0), (0, 0), (0, Q_HEAD_PAD - QK_NOPE - QK_ROPE))
                  ).reshape(depth, q_lora, -1).astype(BF16)
    wukv, w_o_mla_b, w_conv_out_b = w_ukv.astype(BF16), w_o_mla.astype(BF16), w_conv_out.astype(BF16)
    w_mem_kv_b, w_o_mem_b, w_out_b = w_mem_kv.astype(BF16), w_o_mem.astype(BF16), w_out.astype(BF16)
    w_gate_b, w_up_b, w_down_b = w_gate.astype(BF16), w_up.astype(BF16), w_down.astype(BF16)
    conv_w_p = jnp.pad(conv_w, ((0, 0), (0, CONV_HALO - CONV_WIDTH), (0, 0)))
    vec = lambda a: a[:, None, :]
    qg, kvg, conv_b3, conv_g3, conv_beta3 = vec(q_norm_g), vec(kv_norm_g), vec(conv_b), vec(conv_ln_g), vec(conv_ln_b)
    ln1_g3, ln1_b3, ln2_g3, ln2_b3 = vec(ln1_g), vec(ln1_b), vec(ln2_g), vec(ln2_b)

    kvm_all = _matmul_all_layers(mem_bf, w_mem_kv_b, BF16).reshape(depth * batch, -1, 2 * mem_w)

    xf = x.reshape(t, d)
    xb = xf.astype(BF16)
    for l in range(depth):
        q, kv, kr = _mla_proj(l, xb, w_lat, wa_cols, wuq, wukv, qg, kvg, c_tab, s1_tab, s2_tab)
        o_mla = _attention(q.reshape(batch, seq, -1), kv.reshape(batch, seq, -1),
                           kr.reshape(batch, seq, LANES), batch, seq).reshape(t, -1)

        u0, mq = _xproj(l, xb, w_rest, col_a, col_g, col_m, conv_dim)
        u_conv = _conv_module(l, u0.reshape(batch, seq, conv_dim), conv_w_p, conv_b3, conv_g3, conv_beta3,
                              batch, seq).reshape(t, -1)

        o_mem = _mem_attention(l, mq.reshape(batch, seq, mem_w), kvm_all, batch, seq).reshape(t, -1)

        z = _merge(l, xb, o_mla, u_conv, o_mem, w_rest, col_gate, w_o_mla_b, w_conv_out_b, w_o_mem_b)
        x1, x1_packed, route_t = _outproj_ln_route(l, z, w_out_b, xf, ln1_g3, ln1_b3, w_router_t, b_router_col,
                                                   alpha)

        slot_tok, block_e, n_used, pos, wts = _dispatch_plan(route_t, moe_rows)
        ys = _expert_ffn(l, x1_packed, slot_tok, block_e, n_used, w_gate_b, w_up_b, w_down_b, moe_rows)
        xf, xb = _combine_ln(l, ys, pos, wts, x1, ln2_g3, ln2_b3, alpha)
    return xf.reshape(batch, seq, d)
```

```python
import functools

import jax
import jax.numpy as jnp
from jax import lax
from jax.experimental import pallas as pl
from jax.experimental.pallas import tpu as pltpu

F32 = jnp.float32
BF16 = jnp.bfloat16

CHUNK = 64
MLA_HEADS = 16
QK_NOPE = 128
QK_ROPE = 64
V_DIM = 128
ROPE_THETA = 10000.0
MLA_SCALE = (QK_NOPE + QK_ROPE) ** -0.5
CONV_WIDTH = 31
MEM_HEADS = 4
MEM_HEAD_DIM = 256
MEM_SCALE = MEM_HEAD_DIM ** -0.5
N_EXPERTS = 16
N_GROUPS = 4
EXPERTS_PER_GROUP = N_EXPERTS // N_GROUPS
TOP_K = 2
LN_EPS = 1e-5
RMS_EPS = 1e-6

LANES = 128
SUBLANES = 8
Q_HEAD_PAD = 2 * LANES
CONV_HALO = 32
ROUTE_ROWS = 2 * N_EXPERTS
VMEM_LIMIT = 56 * 1024 * 1024
VMEM_LIMIT_OUTPROJ = 60 * 1024 * 1024
NEG_BIG = -1e30
LOG2_E = 1.4426950408889634


def _params(dims, vmem_limit=VMEM_LIMIT):
    return pltpu.CompilerParams(dimension_semantics=dims, vmem_limit_bytes=vmem_limit)


def _blk(dim, pref):
    b = min(dim, pref)
    assert dim % b == 0, (dim, pref)
    return b


def _resident(shape, index_map):
    return pl.BlockSpec(shape, index_map, pipeline_mode=pl.Buffered(1))


def _dot(a, b):
    return jnp.dot(a, b, preferred_element_type=F32)


def _dot_nt(a, b):
    return lax.dot_general(a, b, (((1,), (1,)), ((), ())), preferred_element_type=F32)


def _layer_norm_rows(v, g, b):
    mu = jnp.mean(v, -1, keepdims=True)
    d = v - mu
    var = jnp.mean(d * d, -1, keepdims=True)
    return d * lax.rsqrt(var + LN_EPS) * g + b


CAST_PIECE = QK_ROPE
CAST_PIECES = 4
W_IN_ALIGN = 512


def _cast_t_kernel(*refs, gap_lo, gap_hi):
    o_ref = refs[-1]
    j = pl.program_id(1)
    pieces = []
    for p, r in enumerate(refs[:-1]):
        q = CAST_PIECES * j + p
        pieces.append(jnp.where((q >= gap_lo) & (q < gap_hi), 0.0, r[...]))
    w = jnp.concatenate(pieces, axis=0)
    o_ref[...] = w.T.astype(BF16)


def _cast_transposed(w_t, gap_at, gap):
    depth, n, k = w_t.shape
    width = n + gap
    bn = CAST_PIECE * CAST_PIECES
    assert gap_at % CAST_PIECE == 0 and gap % CAST_PIECE == 0 and width % bn == 0
    gap_lo, gap_hi = gap_at // CAST_PIECE, (gap_at + gap) // CAST_PIECE

    def piece(p):
        def index(l, j):
            q = CAST_PIECES * j + p
            src = jnp.where(q < gap_lo, q, jnp.where(q < gap_hi, gap_lo - 1, q - (gap_hi - gap_lo)))
            return (l, src, 0)
        return pl.BlockSpec((None, CAST_PIECE, k), index)

    return pl.pallas_call(
        functools.partial(_cast_t_kernel, gap_lo=gap_lo, gap_hi=gap_hi),
        grid=(depth, width // bn),
        in_specs=[piece(p) for p in range(CAST_PIECES)],
        out_specs=pl.BlockSpec((None, k, bn), lambda l, j: (l, 0, j)),
        out_shape=jax.ShapeDtypeStruct((depth, k, width), BF16),
        compiler_params=_params(("parallel", "parallel")),
        name="cast_transposed",
    )(*([w_t] * CAST_PIECES))


def _rope_lanes(t, c, s1, s2):
    return t * c + pltpu.roll(t, 32, 1) * s1 + pltpu.roll(t, 96, 1) * s2


def _mla_proj_kernel(x_ref, wa_ref, wuq_ref, wukv_ref, qg_ref, kvg_ref, c_ref, s1_ref, s2_ref,
                     q_ref, kv_ref, kr_ref, *, q_lora, kv_lora):
    h = _dot(x_ref[...], wa_ref[...])
    cq = h[:, :q_lora]
    ckv = h[:, q_lora:q_lora + kv_lora]
    kr = h[:, q_lora + kv_lora:q_lora + kv_lora + LANES]

    def rms(t, g):
        return t * lax.rsqrt(jnp.mean(t * t, -1, keepdims=True) + RMS_EPS) * g

    c, s1, s2 = c_ref[...], s1_ref[...], s2_ref[...]
    kr_ref[...] = _rope_lanes(kr, c, s1, s2).astype(BF16)
    kv_ref[...] = _dot(rms(ckv, kvg_ref[...]).astype(BF16), wukv_ref[...]).astype(BF16)
    q = _dot(rms(cq, qg_ref[...]).astype(BF16), wuq_ref[...])
    for hh in range(MLA_HEADS):
        lo = hh * Q_HEAD_PAD
        q_ref[:, lo:lo + LANES] = q[:, lo:lo + LANES].astype(BF16)
        q_ref[:, lo + LANES:lo + Q_HEAD_PAD] = _rope_lanes(
            q[:, lo + LANES:lo + Q_HEAD_PAD], c, s1, s2).astype(BF16)


def _mla_proj(l, x_bf, w_in_p, wa_cols, wuq, wukv, qg, kvg, c_tab, s1_tab, s2_tab):
    t, d = x_bf.shape
    q_lora, kv_lora = qg.shape[2], kvg.shape[2]
    nq, nkv = wuq.shape[2], wukv.shape[2]
    bm = _blk(t, 256)
    row = lambda i: (i, 0)
    layer = lambda i: (l, 0, 0)
    return pl.pallas_call(
        functools.partial(_mla_proj_kernel, q_lora=q_lora, kv_lora=kv_lora),
        grid=(t // bm,),
        in_specs=[
            pl.BlockSpec((bm, d), row),
            _resident((None, d, wa_cols), layer),
            _resident((None, q_lora, nq), layer),
            _resident((None, kv_lora, nkv), layer),
            _resident((None, 1, q_lora), layer),
            _resident((None, 1, kv_lora), layer),
            pl.BlockSpec((bm, LANES), row),
            pl.BlockSpec((bm, LANES), row),
            pl.BlockSpec((bm, LANES), row),
        ],
        out_specs=[
            pl.BlockSpec((bm, nq), row),
            pl.BlockSpec((bm, nkv), row),
            pl.BlockSpec((bm, LANES), row),
        ],
        out_shape=[
            jax.ShapeDtypeStruct((t, nq), BF16),
            jax.ShapeDtypeStruct((t, nkv), BF16),
            jax.ShapeDtypeStruct((t, LANES), BF16),
        ],
        compiler_params=_params(("parallel",)),
        name="mla_proj",
    )(x_bf, w_in_p, wuq, wukv, qg, kvg, c_tab, s1_tab, s2_tab)


def _attn_kernel(q_ref, kv_ref, kr_ref, o_ref, kcat_ref, vaug_ref, *, seq, tq):
    kcat_ref[:, :LANES] = kv_ref[0, :, :LANES]
    kcat_ref[:, LANES:] = kr_ref[0]
    vaug_ref[:, :LANES] = kv_ref[0, :, LANES:]
    ones_col = lax.broadcasted_iota(jnp.int32, (seq, LANES), 1) == 0
    vaug_ref[:, LANES:] = jnp.where(ones_col, 1.0, 0.0).astype(BF16)
    rq = lax.broadcasted_iota(jnp.int32, (tq, tq), 0) // CHUNK
    ck = lax.broadcasted_iota(jnp.int32, (tq, tq), 1) // CHUNK
    diag_ok = ck <= rq
    expo = MLA_SCALE * LOG2_E
    for i in range(seq // tq):
        lo = i * tq
        q = q_ref[0, lo:lo + tq, :]
        s_d = jnp.where(diag_ok, _dot_nt(q, kcat_ref[lo:lo + tq, :]), NEG_BIG)
        m = jnp.max(s_d, -1, keepdims=True)
        if i > 0:
            s_f = _dot_nt(q, kcat_ref[:lo, :])
            m = jnp.maximum(m, jnp.max(s_f, -1, keepdims=True))
        o = _dot(jnp.exp2((s_d - m) * expo).astype(BF16), vaug_ref[lo:lo + tq, :])
        if i > 0:
            o = o + _dot(jnp.exp2((s_f - m) * expo).astype(BF16), vaug_ref[:lo, :])
        o_ref[0, lo:lo + tq, :] = (o[:, :LANES] / o[:, LANES:LANES + 1]).astype(BF16)


def _attention(q, kv, kr, batch, seq):
    tq = _blk(seq, 256)
    return pl.pallas_call(
        functools.partial(_attn_kernel, seq=seq, tq=tq),
        grid=(batch, MLA_HEADS),
        in_specs=[
            pl.BlockSpec((1, seq, Q_HEAD_PAD), lambda b, h: (b, 0, h)),
            pl.BlockSpec((1, seq, QK_NOPE + V_DIM), lambda b, h: (b, 0, h)),
            pl.BlockSpec((1, seq, LANES), lambda b, h: (b, 0, 0)),
        ],
        out_specs=pl.BlockSpec((1, seq, V_DIM), lambda b, h: (b, 0, h)),
        out_shape=jax.ShapeDtypeStruct((batch, seq, MLA_HEADS * V_DIM), BF16),
        scratch_shapes=[pltpu.VMEM((seq, Q_HEAD_PAD), BF16), pltpu.VMEM((seq, V_DIM + LANES), BF16)],
        compiler_params=_params(("parallel", "parallel")),
        name="mla_attention",
    )(q, kv, kr)


def _xproj_kernel(x_ref, wa_ref, wg_ref, wm_ref, u_ref, mq_ref):
    x = x_ref[...]
    u_ref[...] = _dot(x, wa_ref[...]) * jax.nn.sigmoid(_dot(x, wg_ref[...]))
    mq_ref[...] = _dot(x, wm_ref[...]).astype(BF16)


def _xproj(l, x_bf, w_in_p, col_a, col_g, col_m, n):
    t, d = x_bf.shape
    bm, bn = _blk(t, 512), _blk(n, 512)
    wspec = lambda col: pl.BlockSpec((None, d, bn), lambda j, i: (l, 0, col // bn + j))
    ospec = pl.BlockSpec((bm, bn), lambda j, i: (i, j))
    return pl.pallas_call(
        _xproj_kernel,
        grid=(n // bn, t // bm),
        in_specs=[pl.BlockSpec((bm, d), lambda j, i: (i, 0)), wspec(col_a), wspec(col_g), wspec(col_m)],
        out_specs=[ospec, ospec],
        out_shape=[jax.ShapeDtypeStruct((t, n), F32), jax.ShapeDtypeStruct((t, n), BF16)],
        compiler_params=_params(("parallel", "parallel")),
        name="xproj_glu_memq",
    )(x_bf, w_in_p, w_in_p, w_in_p)


def _mm_kernel(a_ref, w_ref, o_ref):
    o_ref[...] = _dot(a_ref[...], w_ref[...]).astype(o_ref.dtype)


def _matmul_all_layers(a, w, out_dtype):
    m, k = a.shape
    depth, _, n = w.shape
    bm, bn = _blk(m, 512), _blk(n, 512)
    return pl.pallas_call(
        _mm_kernel,
        grid=(depth, n // bn, m // bm),
        in_specs=[pl.BlockSpec((bm, k), lambda l, j, i: (i, 0)),
                  pl.BlockSpec((None, k, bn), lambda l, j, i: (l, 0, j))],
        out_specs=pl.BlockSpec((None, bm, bn), lambda l, j, i: (l, i, j)),
        out_shape=jax.ShapeDtypeStruct((depth, m, n), out_dtype),
        compiler_params=_params(("parallel", "parallel", "parallel")),
        name="matmul_all_layers",
    )(a, w)


def _conv_kernel(cur_ref, prev_ref, w_ref, b_ref, g_ref, beta_ref, o_ref, win_ref, *, ts, rt):
    i = pl.program_id(1)
    win_ref[0:CONV_HALO, :] = jnp.where(i > 0, prev_ref[0, ts - CONV_HALO:ts, :], 0.0)
    win_ref[CONV_HALO:CONV_HALO + ts, :] = cur_ref[0]
    win_ref[CONV_HALO + ts:, :] = jnp.zeros((SUBLANES, win_ref.shape[1]), F32)
    first_tap = CONV_HALO - (CONV_WIDTH - 1)
    for r in range(ts // rt):
        r0 = r * rt
        acc = None
        for phase in range(SUBLANES):
            ys = None
            for k in range(CONV_WIDTH):
                off = first_tap + k
                if off % SUBLANES != phase:
                    continue
                base = r0 + off - phase
                term = win_ref[base:base + rt + SUBLANES, :] * w_ref[k:k + 1, :]
                ys = term if ys is None else ys + term
            part = ys[phase:phase + rt, :]
            acc = part if acc is None else acc + part
        v = _layer_norm_rows(acc + b_ref[...], g_ref[...], beta_ref[...])
        o_ref[0, r0:r0 + rt, :] = (v * jax.nn.sigmoid(v)).astype(BF16)


def _conv_module(l, u0, conv_w, conv_b, ln_g, ln_b, batch, seq):
    c = u0.shape[-1]
    ts = _blk(seq, 256)
    assert ts >= CONV_HALO
    rt = _blk(ts, 32)
    vec = lambda b, i: (l, 0, 0)
    return pl.pallas_call(
        functools.partial(_conv_kernel, ts=ts, rt=rt),
        grid=(batch, seq // ts),
        in_specs=[
            pl.BlockSpec((1, ts, c), lambda b, i: (b, i, 0)),
            pl.BlockSpec((1, ts, c), lambda b, i: (b, jnp.maximum(i - 1, 0), 0)),
            pl.BlockSpec((None, CONV_HALO, c), vec),
            pl.BlockSpec((None, 1, c), vec),
            pl.BlockSpec((None, 1, c), vec),
            pl.BlockSpec((None, 1, c), vec),
        ],
        out_specs=pl.BlockSpec((1, ts, c), lambda b, i: (b, i, 0)),
        out_shape=jax.ShapeDtypeStruct((batch, seq, c), BF16),
        scratch_shapes=[pltpu.VMEM((CONV_HALO + ts + SUBLANES, c), F32)],
        compiler_params=_params(("parallel", "arbitrary")),
        name="conv_module",
    )(u0, u0, conv_w, conv_b, ln_g, ln_b)


def _mem_attn_kernel(q_ref, kv_ref, o_ref):
    width = MEM_HEADS * MEM_HEAD_DIM
    for h in range(MEM_HEADS):
        lo = h * MEM_HEAD_DIM
        s = _dot_nt(q_ref[0, :, lo:lo + MEM_HEAD_DIM], kv_ref[0, :, lo:lo + MEM_HEAD_DIM]) * MEM_SCALE
        p = jnp.exp(s - jnp.max(s, -1, keepdims=True))
        l = jnp.sum(p, -1, keepdims=True)
        o = _dot(p.astype(BF16), kv_ref[0, :, width + lo:width + lo + MEM_HEAD_DIM])
        o_ref[0, :, lo:lo + MEM_HEAD_DIM] = (o / l).astype(BF16)


def _mem_attention(l, mq, kvm, batch, seq):
    width = MEM_HEADS * MEM_HEAD_DIM
    m = kvm.shape[1]
    ts = _blk(seq, 512)
    return pl.pallas_call(
        _mem_attn_kernel,
        grid=(batch, seq // ts),
        in_specs=[
            pl.BlockSpec((1, ts, width), lambda b, i: (b, i, 0)),
            pl.BlockSpec((1, m, 2 * width), lambda b, i: (l * batch + b, 0, 0)),
        ],
        out_specs=pl.BlockSpec((1, ts, width), lambda b, i: (b, i, 0)),
        out_shape=jax.ShapeDtypeStruct((batch, seq, width), BF16),
        compiler_params=_params(("parallel", "parallel")),
        name="mem_attention",
    )(mq, kvm)


def _merge_kernel(x_ref, o_ref, u_ref, m_ref, wg0_ref, wg1_ref, wg2_ref, w0_ref, w1_ref, w2_ref, z_ref):
    x = x_ref[...]
    z = jax.nn.sigmoid(_dot(x, wg0_ref[...])) * _dot(o_ref[...], w0_ref[...])
    z = z + jax.nn.sigmoid(_dot(x, wg1_ref[...])) * _dot(u_ref[...], w1_ref[...])
    z = z + jax.nn.sigmoid(_dot(x, wg2_ref[...])) * _dot(m_ref[...], w2_ref[...])
    z_ref[...] = z.astype(BF16)


def _merge(l, x_bf, o_mla, u_conv, o_mem, w_in_p, col_gate, w_o_mla, w_conv_out, w_o_mem):
    t, d = x_bf.shape
    bm, bn = _blk(t, 512), _blk(d, 256)
    nj = d // bn
    act = lambda a: pl.BlockSpec((bm, a.shape[1]), lambda j, i: (i, 0))
    gate = lambda b: pl.BlockSpec((None, d, bn), lambda j, i, b=b: (l, 0, col_gate // bn + b * nj + j))
    wbr = lambda w: pl.BlockSpec((None, w.shape[1], bn), lambda j, i: (l, 0, j))
    return pl.pallas_call(
        _merge_kernel,
        grid=(nj, t // bm),
        in_specs=[act(x_bf), act(o_mla), act(u_conv), act(o_mem), gate(0), gate(1), gate(2),
                  wbr(w_o_mla), wbr(w_conv_out), wbr(w_o_mem)],
        out_specs=pl.BlockSpec((bm, bn), lambda j, i: (i, j)),
        out_shape=jax.ShapeDtypeStruct((t, d), BF16),
        compiler_params=_params(("parallel", "parallel")),
        name="gated_merge",
    )(x_bf, o_mla, u_conv, o_mem, w_in_p, w_in_p, w_in_p, w_o_mla, w_conv_out, w_o_mem)


def _route_rows(logits, b_router):
    scores = jax.nn.sigmoid(logits[:N_EXPERTS])
    biased = scores + b_router[:N_EXPERTS]
    sc = [scores[e:e + 1] for e in range(N_EXPERTS)]
    bg = [biased[e:e + 1] for e in range(N_EXPERTS)]
    group_score = []
    for g in range(N_GROUPS):
        v = bg[g * EXPERTS_PER_GROUP:(g + 1) * EXPERTS_PER_GROUP]
        best = None
        for a in range(EXPERTS_PER_GROUP):
            for b in range(a + 1, EXPERTS_PER_GROUP):
                pair = v[a] + v[b]
                best = pair if best is None else jnp.maximum(best, pair)
        group_score.append(best)
    best, best_g = group_score[0], jnp.zeros_like(group_score[0], dtype=jnp.int32)
    for g in range(1, N_GROUPS):
        better = group_score[g] > best
        best = jnp.where(better, group_score[g], best)
        best_g = jnp.where(better, g, best_g)
    sel, picked = [], []
    for e in range(N_EXPERTS):
        g, j = divmod(e, EXPERTS_PER_GROUP)
        rank = jnp.zeros_like(best_g)
        for jj in range(EXPERTS_PER_GROUP):
            if jj == j:
                continue
            other = bg[g * EXPERTS_PER_GROUP + jj]
            ahead = (other > bg[e]) | ((other == bg[e]) & (jj < j))
            rank = rank + ahead.astype(jnp.int32)
        s = (best_g == g) & (rank < TOP_K)
        sel.append(s)
        picked.append(jnp.where(s, sc[e], 0.0))
    denom = picked[0]
    for e in range(1, N_EXPERTS):
        denom = denom + picked[e]
    return [p / denom for p in picked], sel


def _pack_bf16_pair(lo, hi):
    lo_bits = lax.bitcast_convert_type(lo.astype(F32), jnp.uint32) >> 16
    hi_bits = lax.bitcast_convert_type(hi.astype(F32), jnp.uint32) & jnp.uint32(0xFFFF0000)
    return hi_bits | lo_bits


def _unpack_bf16_pair(words):
    lo = lax.bitcast_convert_type(words << 16, F32).astype(BF16)
    hi = lax.bitcast_convert_type(words & jnp.uint32(0xFFFF0000), F32).astype(BF16)
    return lo, hi


def _outproj_ln_kernel(z_ref, w_ref, x_ref, g_ref, b_ref, wr_ref, br_ref,
                       xo_ref, xpk_ref, route_ref, acc_ref, *, alpha, nj, bn):
    j = pl.program_id(1)
    acc_ref[j] = alpha * x_ref[...] + _dot(z_ref[...], w_ref[...])

    @pl.when(j == nj - 1)
    def _():
        d = nj * bn
        tot = jnp.sum(acc_ref[0], -1, keepdims=True)
        for jj in range(1, nj):
            tot = tot + jnp.sum(acc_ref[jj], -1, keepdims=True)
        mu = tot / d
        sq = None
        for jj in range(nj):
            dv = acc_ref[jj] - mu
            part = jnp.sum(dv * dv, -1, keepdims=True)
            sq = part if sq is None else sq + part
        rstd = lax.rsqrt(sq / d + LN_EPS)
        logits = None
        half = nj // 2
        for jj in range(half):
            pair = []
            for c in (jj, jj + half):
                lo = c * bn
                v = (acc_ref[c] - mu) * rstd * g_ref[:, lo:lo + bn] + b_ref[:, lo:lo + bn]
                xo_ref[:, lo:lo + bn] = v
                vb = v.astype(BF16)
                part = _dot_nt(wr_ref[:, lo:lo + bn], vb)
                logits = part if logits is None else logits + part
                pair.append(vb)
            xpk_ref[:, jj * bn:(jj + 1) * bn] = _pack_bf16_pair(pair[0], pair[1])
        cw, sel = _route_rows(logits, br_ref[...])
        for e in range(N_EXPERTS):
            route_ref[e:e + 1, :] = cw[e]
            route_ref[N_EXPERTS + e:N_EXPERTS + e + 1, :] = sel[e].astype(F32)


def _outproj_ln_route(l, z, w_out, x_res, ln_g, ln_b, w_router_t, b_router_col, alpha):
    t, d = z.shape
    bm, bn = _blk(t, 512), _blk(d // 2, 512)
    nj = d // bn
    fixed = lambda i, j: (0, 0)
    layer = lambda i, j: (l, 0, 0)
    return pl.pallas_call(
        functools.partial(_outproj_ln_kernel, alpha=alpha, nj=nj, bn=bn),
        grid=(t // bm, nj),
        in_specs=[
            pl.BlockSpec((bm, d), lambda i, j: (i, 0)),
            pl.BlockSpec((None, d, bn), lambda i, j: (l, 0, j)),
            pl.BlockSpec((bm, bn), lambda i, j: (i, j)),
            _resident((None, 1, d), layer),
            _resident((None, 1, d), layer),
            _resident((LANES, d), fixed),
            _resident((LANES, 1), fixed),
        ],
        out_specs=[
            pl.BlockSpec((bm, d), lambda i, j: (i, 0)),
            pl.BlockSpec((bm, d // 2), lambda i, j: (i, 0)),
            pl.BlockSpec((ROUTE_ROWS, bm), lambda i, j: (0, i)),
        ],
        out_shape=[
            jax.ShapeDtypeStruct((t, d), F32),
            jax.ShapeDtypeStruct((t, d // 2), jnp.uint32),
            jax.ShapeDtypeStruct((ROUTE_ROWS, t), F32),
        ],
        scratch_shapes=[pltpu.VMEM((nj, bm, bn), F32)],
        compiler_params=_params(("parallel", "arbitrary"), VMEM_LIMIT_OUTPROJ),
        name="outproj_ln_route",
    )(z, w_out, x_res, ln_g, ln_b, w_router_t, b_router_col)


def _row_copy(src_hbm, row, dst_ref, r, sem):
    return pltpu.make_async_copy(src_hbm.at[pl.ds(row, 1), :], dst_ref.at[pl.ds(r, 1), :], sem)


FFN_BUFFERS = 3


def _ffn_kernel(be_ref, nused_ref, tok_a, tok_b, tok_c, x_hbm, wg_ref, wu_ref, wd_ref, y_ref,
                xbuf0, xbuf1, xbuf2, sems, *, rows):
    i = pl.program_id(0)
    n_used = nused_ref[0]
    bufs = (xbuf0, xbuf1, xbuf2)
    half = wg_ref.shape[0] // 2

    def gather(tok_ref, s):
        for r in range(rows):
            _row_copy(x_hbm, tok_ref[0, 0, r], bufs[s], r, sems.at[s]).start()

    @pl.when(i == 0)
    def _():
        gather(tok_a, 0)
        gather(tok_b, 1)

    @pl.when(i >= n_used)
    def _():
        y_ref[...] = jnp.zeros(y_ref.shape, y_ref.dtype)

    for s in range(FFN_BUFFERS):
        mine = i % FFN_BUFFERS == s

        @pl.when(mine & (i < n_used + 2))
        def _():
            for r in range(rows):
                _row_copy(x_hbm, 0, bufs[s], r, sems.at[s]).wait()

        @pl.when(mine & (i < n_used))
        def _():
            gather(tok_c, (s + 2) % FFN_BUFFERS)
            x_lo, x_hi = _unpack_bf16_pair(bufs[s][...])
            g = _dot(x_lo, wg_ref[:half, :]) + _dot(x_hi, wg_ref[half:, :])
            u = _dot(x_lo, wu_ref[:half, :]) + _dot(x_hi, wu_ref[half:, :])
            h = (g * jax.nn.sigmoid(g)) * u
            y_ref[...] = _dot(h.astype(BF16), wd_ref[...])


def _expert_ffn(l, x_packed, slot_tok, block_e, n_used, w_gate, w_up, w_down, rows):
    p = slot_tok.shape[0]
    dh = x_packed.shape[1]
    d = 2 * dh
    de = w_gate.shape[3]
    nb = p // rows
    tok = slot_tok.reshape(nb, 1, rows)
    wsel = lambda i, be, nu: (l, be[jnp.minimum(i, nu[0] - 1)], 0, 0)
    tok_block = lambda f: pl.BlockSpec((1, 1, rows), lambda i, be, nu: (f(i), 0, 0), memory_space=pltpu.SMEM)
    return pl.pallas_call(
        functools.partial(_ffn_kernel, rows=rows),
        grid_spec=pltpu.PrefetchScalarGridSpec(
            num_scalar_prefetch=2,
            grid=(nb,),
            in_specs=[
                tok_block(lambda i: 0),
                tok_block(lambda i: 1),
                tok_block(lambda i: jnp.minimum(i + 2, nb - 1)),
                pl.BlockSpec(memory_space=pl.ANY),
                pl.BlockSpec((None, None, d, de), wsel),
                pl.BlockSpec((None, None, d, de), wsel),
                pl.BlockSpec((None, None, de, d), wsel),
            ],
            out_specs=pl.BlockSpec((rows, d), lambda i, be, nu: (i, 0)),
            scratch_shapes=[pltpu.VMEM((rows, dh), jnp.uint32)] * FFN_BUFFERS
            + [pltpu.SemaphoreType.DMA((FFN_BUFFERS,))],
        ),
        out_shape=jax.ShapeDtypeStruct((p, d), F32),
        compiler_params=_params(("arbitrary",)),
        name="moe_expert_ffn",
    )(block_e, n_used, tok, tok, tok, x_packed, w_gate, w_up, w_down)


def _combine_kernel(pos0_ref, pos1_ref, ys_hbm, x_ref, wts_ref, g_ref, b_ref, xo_ref, xbf_ref,
                    a0, b0, a1, b1, sems, *, alpha, rows, nb):
    i = pl.program_id(0)
    bufs = ((a0, b0), (a1, b1))

    def gather(pos_ref, s):
        for r in range(rows):
            _row_copy(ys_hbm, pos_ref[0, 0, r], bufs[s][0], r, sems.at[s]).start()
            _row_copy(ys_hbm, pos_ref[0, 1, r], bufs[s][1], r, sems.at[s]).start()

    def drain(s):
        for r in range(rows):
            _row_copy(ys_hbm, 0, bufs[s][0], r, sems.at[s]).wait()
            _row_copy(ys_hbm, 0, bufs[s][1], r, sems.at[s]).wait()

    @pl.when(i == 0)
    def _():
        gather(pos0_ref, 0)

    for s in range(2):
        @pl.when(i % 2 == s)
        def _():
            drain(s)
            gather(pos1_ref, 1 - s)
            y = wts_ref[:, 0:1] * bufs[s][0][...] + wts_ref[:, 1:2] * bufs[s][1][...]
            v = _layer_norm_rows(alpha * x_ref[...] + y, g_ref[...], b_ref[...])
            xo_ref[...] = v
            xbf_ref[...] = v.astype(BF16)

        @pl.when((i % 2 == s) & (i == nb - 1))
        def _():
            drain(1 - s)


def _combine_ln(l, ys, pos, wts, x_res, ln_g, ln_b, alpha):
    t, d = x_res.shape
    rows = _blk(t, 128)
    nb = t // rows
    layer = lambda i: (l, 0, 0)
    pos_blocks = pos.reshape(nb, rows, 2).transpose(0, 2, 1)
    return pl.pallas_call(
        functools.partial(_combine_kernel, alpha=alpha, rows=rows, nb=nb),
        grid=(nb,),
        in_specs=[
            pl.BlockSpec((1, 2, rows), lambda i: (0, 0, 0), memory_space=pltpu.SMEM),
            pl.BlockSpec((1, 2, rows), lambda i: (jnp.minimum(i + 1, nb - 1), 0, 0), memory_space=pltpu.SMEM),
            pl.BlockSpec(memory_space=pl.ANY),
            pl.BlockSpec((rows, d), lambda i: (i, 0)),
            pl.BlockSpec((rows, LANES), lambda i: (i, 0)),
            _resident((None, 1, d), layer),
            _resident((None, 1, d), layer),
        ],
        out_specs=[pl.BlockSpec((rows, d), lambda i: (i, 0)), pl.BlockSpec((rows, d), lambda i: (i, 0))],
        out_shape=[jax.ShapeDtypeStruct((t, d), F32), jax.ShapeDtypeStruct((t, d), BF16)],
        scratch_shapes=[pltpu.VMEM((rows, d), F32)] * 4 + [pltpu.SemaphoreType.DMA((2,))],
        compiler_params=_params(("arbitrary",)),
        name="moe_combine_ln",
    )(pos_blocks, pos_blocks, ys, x_res, wts, ln_g, ln_b)


def _dispatch_plan(route_t, rows):
    t = route_t.shape[1]
    cw = route_t[:N_EXPERTS].T
    sel = route_t[N_EXPERTS:].T > 0.5
    sel_i = sel.astype(jnp.int32)
    csum = jnp.cumsum(sel_i, axis=0)
    counts = csum[-1]
    padded = (counts + rows - 1) // rows * rows
    seg_end = jnp.cumsum(padded)
    dest = (seg_end - padded)[None, :] + (csum - sel_i)
    nb = t * TOP_K // rows + N_EXPERTS + FFN_BUFFERS - 1
    n_slots = nb * rows
    pos0 = jnp.min(jnp.where(sel, dest, n_slots), axis=1)
    pos1 = jnp.max(jnp.where(sel, dest, -1), axis=1)
    w0 = jnp.sum(jnp.where(sel & (dest == pos0[:, None]), cw, 0.0), axis=1)
    w1 = jnp.sum(jnp.where(sel & (dest == pos1[:, None]), cw, 0.0), axis=1)
    tok = jnp.arange(t, dtype=jnp.int32)
    slot_tok = jnp.zeros((n_slots,), jnp.int32).at[jnp.concatenate([pos0, pos1])].set(
        jnp.concatenate([tok, tok]), unique_indices=True)
    block_start = jnp.arange(nb, dtype=jnp.int32) * rows
    block_e = jnp.minimum(jnp.sum((seg_end[None, :] <= block_start[:, None]).astype(jnp.int32), axis=1),
                          N_EXPERTS - 1).astype(jnp.int32)
    n_used = (seg_end[-1] // rows).astype(jnp.int32).reshape(1)
    pos = jnp.stack([pos0, pos1], axis=1).astype(jnp.int32)
    wts = jnp.pad(jnp.stack([w0, w1], axis=1), ((0, 0), (0, LANES - TOP_K)))
    return slot_tok, block_e, n_used, pos, wts


def _rope_tables(positions):
    inv = 1.0 / (ROPE_THETA ** (jnp.arange(0, QK_ROPE, 2, dtype=F32) / QK_ROPE))
    ang = positions.astype(F32).reshape(-1, 1) * inv
    cos, sin = jnp.cos(ang), jnp.sin(ang)
    zero = jnp.zeros_like(cos)
    pad = jnp.zeros((cos.shape[0], LANES - QK_ROPE), F32)
    c_tab = jnp.concatenate([cos, cos, pad], 1)
    s1_tab = jnp.concatenate([zero, sin, pad], 1)
    s2_tab = jnp.concatenate([-sin, zero, pad], 1)
    return c_tab, s1_tab, s2_tab


def kernel(x, mem, positions, w_in, q_norm_g, kv_norm_g, w_uq, w_ukv, w_o_mla, conv_w, conv_b, conv_ln_g, conv_ln_b, w_conv_out, w_mem_kv, w_o_mem, w_out, ln1_g, ln1_b, w_router, b_router, w_gate, w_up, w_down, ln2_g, ln2_b):
    batch, seq, d = x.shape
    depth = w_in.shape[0]
    t = batch * seq
    q_lora, kv_lora, conv_dim = q_norm_g.shape[1], kv_norm_g.shape[1], conv_w.shape[2]
    mem_w = MEM_HEADS * MEM_HEAD_DIM
    assert conv_dim == mem_w
    alpha = (2 * depth) ** 0.25
    moe_rows = 256

    c_tab, s1_tab, s2_tab = _rope_tables(positions)
    mem_bf = mem.reshape(-1, d).astype(BF16)
    w_router_t = jnp.pad(w_router.T, ((0, LANES - N_EXPERTS), (0, 0))).astype(BF16)
    b_router_col = jnp.pad(b_router.astype(F32), (0, LANES - N_EXPERTS)).reshape(LANES, 1)

    o_conv = q_lora + kv_lora + QK_ROPE
    wa_cols = -(-(o_conv + LANES - QK_ROPE) // W_IN_ALIGN) * W_IN_ALIGN
    w_all = _cast_transposed(jnp.swapaxes(w_in, 1, 2), o_conv, wa_cols - o_conv)
    col_a, col_g, col_m = wa_cols, wa_cols + conv_dim, wa_cols + 2 * conv_dim
    col_gate = col_m + mem_w

    wuq = jnp.pad(w_uq.reshape(depth, q_lora, MLA_HEADS, QK_NOPE + QK_ROPE),
                  ((0, 0), (0, 0), (0, 0), (0, Q_HEAD_PAD - QK_NOPE - QK_ROPE))
                  ).reshape(depth, q_lora, -1).astype(BF16)
    wukv, w_o_mla_b, w_conv_out_b = w_ukv.astype(BF16), w_o_mla.astype(BF16), w_conv_out.astype(BF16)
    w_mem_kv_b, w_o_mem_b, w_out_b = w_mem_kv.astype(BF16), w_o_mem.astype(BF16), w_out.astype(BF16)
    w_gate_b, w_up_b, w_down_b = w_gate.astype(BF16), w_up.astype(BF16), w_down.astype(BF16)
    conv_w_p = jnp.pad(conv_w, ((0, 0), (0, CONV_HALO - CONV_WIDTH), (0, 0)))
    vec = lambda a: a[:, None, :]
    qg, kvg, conv_b3, conv_g3, conv_beta3 = vec(q_norm_g), vec(kv_norm_g), vec(conv_b), vec(conv_ln_g), vec(conv_ln_b)
    ln1_g3, ln1_b3, ln2_g3, ln2_b3 = vec(ln1_g), vec(ln1_b), vec(ln2_g), vec(ln2_b)

    kvm_all = _matmul_all_layers(mem_bf, w_mem_kv_b, BF16).reshape(depth * batch, -1, 2 * mem_w)

    xf = x.reshape(t, d)
    xb = xf.astype(BF16)
    for l in range(depth):
        q, kv, kr = _mla_proj(l, xb, w_all, wa_cols, wuq, wukv, qg, kvg, c_tab, s1_tab, s2_tab)
        o_mla = _attention(q.reshape(batch, seq, -1), kv.reshape(batch, seq, -1),
                           kr.reshape(batch, seq, LANES), batch, seq).reshape(t, -1)

        u0, mq = _xproj(l, xb, w_all, col_a, col_g, col_m, conv_dim)
        u_conv = _conv_module(l, u0.reshape(batch, seq, conv_dim), conv_w_p, conv_b3, conv_g3, conv_beta3,
                              batch, seq).reshape(t, -1)

        o_mem = _mem_attention(l, mq.reshape(batch, seq, mem_w), kvm_all, batch, seq).reshape(t, -1)

        z = _merge(l, xb, o_mla, u_conv, o_mem, w_all, col_gate, w_o_mla_b, w_conv_out_b, w_o_mem_b)
        x1, x1_packed, route_t = _outproj_ln_route(l, z, w_out_b, xf, ln1_g3, ln1_b3, w_router_t, b_router_col,
                                                   alpha)

        slot_tok, block_e, n_used, pos, wts = _dispatch_plan(route_t, moe_rows)
        ys = _expert_ffn(l, x1_packed, slot_tok, block_e, n_used, w_gate_b, w_up_b, w_down_b, moe_rows)
        xf, xb = _combine_ln(l, ys, pos, wts, x1, ln2_g3, ln2_b3, alpha)
    return xf.reshape(batch, seq, d)
```

```python
import functools

import jax
import jax.numpy as jnp
from jax import lax
from jax.experimental import pallas as pl
from jax.experimental.pallas import tpu as pltpu

F32 = jnp.float32
BF16 = jnp.bfloat16

CHUNK = 64
MLA_HEADS = 16
QK_NOPE = 128
QK_ROPE = 64
V_DIM = 128
ROPE_THETA = 10000.0
MLA_SCALE = (QK_NOPE + QK_ROPE) ** -0.5
CONV_WIDTH = 31
MEM_HEADS = 4
MEM_HEAD_DIM = 256
MEM_SCALE = MEM_HEAD_DIM ** -0.5
N_EXPERTS = 16
N_GROUPS = 4
EXPERTS_PER_GROUP = N_EXPERTS // N_GROUPS
TOP_K = 2
LN_EPS = 1e-5
RMS_EPS = 1e-6

LANES = 128
SUBLANES = 8
Q_HEAD_PAD = 2 * LANES
CONV_HALO = 32
ROUTE_ROWS = 2 * N_EXPERTS
VMEM_LIMIT = 56 * 1024 * 1024
VMEM_LIMIT_OUTPROJ = 60 * 1024 * 1024
NEG_BIG = -1e30
LOG2_E = 1.4426950408889634


def _params(dims, vmem_limit=VMEM_LIMIT):
    return pltpu.CompilerParams(dimension_semantics=dims, vmem_limit_bytes=vmem_limit)


def _blk(dim, pref):
    b = min(dim, pref)
    assert dim % b == 0, (dim, pref)
    return b


def _resident(shape, index_map):
    return pl.BlockSpec(shape, index_map, pipeline_mode=pl.Buffered(1))


def _dot(a, b):
    return jnp.dot(a, b, preferred_element_type=F32)


def _dot_nt(a, b):
    return lax.dot_general(a, b, (((1,), (1,)), ((), ())), preferred_element_type=F32)


def _layer_norm_rows(v, g, b):
    mu = jnp.mean(v, -1, keepdims=True)
    d = v - mu
    var = jnp.mean(d * d, -1, keepdims=True)
    return d * lax.rsqrt(var + LN_EPS) * g + b


CAST_PIECE = QK_ROPE
CAST_PIECES = 4
W_IN_ALIGN = 512


def _cast_t_kernel(*refs, gap_lo, gap_hi):
    o_ref = refs[-1]
    j = pl.program_id(1)
    pieces = []
    for p, r in enumerate(refs[:-1]):
        q = CAST_PIECES * j + p
        pieces.append(jnp.where((q >= gap_lo) & (q < gap_hi), 0.0, r[...]))
    w = jnp.concatenate(pieces, axis=0)
    o_ref[...] = w.T.astype(BF16)


def _cast_transposed(w_t, gap_at, gap):
    depth, n, k = w_t.shape
    width = n + gap
    bn = CAST_PIECE * CAST_PIECES
    assert gap_at % CAST_PIECE == 0 and gap % CAST_PIECE == 0 and width % bn == 0
    gap_lo, gap_hi = gap_at // CAST_PIECE, (gap_at + gap) // CAST_PIECE

    def piece(p):
        def index(l, j):
            q = CAST_PIECES * j + p
            src = jnp.where(q < gap_lo, q, jnp.where(q < gap_hi, gap_lo - 1, q - (gap_hi - gap_lo)))
            return (l, src, 0)
        return pl.BlockSpec((None, CAST_PIECE, k), index)

    return pl.pallas_call(
        functools.partial(_cast_t_kernel, gap_lo=gap_lo, gap_hi=gap_hi),
        grid=(depth, width // bn),
        in_specs=[piece(p) for p in range(CAST_PIECES)],
        out_specs=pl.BlockSpec((None, k, bn), lambda l, j: (l, 0, j)),
        out_shape=jax.ShapeDtypeStruct((depth, k, width), BF16),
        compiler_params=_params(("parallel", "parallel")),
        name="cast_transposed",
    )(*([w_t] * CAST_PIECES))


def _rope_lanes(t, c, s1, s2):
    return t * c + pltpu.roll(t, 32, 1) * s1 + pltpu.roll(t, 96, 1) * s2


def _mla_proj_kernel(x_ref, wa_ref, wuq_ref, wukv_ref, qg_ref, kvg_ref, c_ref, s1_ref, s2_ref,
                     q_ref, kv_ref, kr_ref, *, q_lora, kv_lora):
    h = _dot(x_ref[...], wa_ref[...])
    cq = h[:, :q_lora]
    ckv = h[:, q_lora:q_lora + kv_lora]
    kr = h[:, q_lora + kv_lora:q_lora + kv_lora + LANES]

    def rms(t, g):
        return t * lax.rsqrt(jnp.mean(t * t, -1, keepdims=True) + RMS_EPS) * g

    c, s1, s2 = c_ref[...], s1_ref[...], s2_ref[...]
    kr_ref[...] = _rope_lanes(kr, c, s1, s2).astype(BF16)
    kv_ref[...] = _dot(rms(ckv, kvg_ref[...]).astype(BF16), wukv_ref[...]).astype(BF16)
    q = _dot(rms(cq, qg_ref[...]).astype(BF16), wuq_ref[...])
    for hh in range(MLA_HEADS):
        lo = hh * Q_HEAD_PAD
        q_ref[:, lo:lo + LANES] = q[:, lo:lo + LANES].astype(BF16)
        q_ref[:, lo + LANES:lo + Q_HEAD_PAD] = _rope_lanes(
            q[:, lo + LANES:lo + Q_HEAD_PAD], c, s1, s2).astype(BF16)


def _mla_proj(l, x_bf, w_in_p, wa_cols, wuq, wukv, qg, kvg, c_tab, s1_tab, s2_tab):
    t, d = x_bf.shape
    q_lora, kv_lora = qg.shape[2], kvg.shape[2]
    nq, nkv = wuq.shape[2], wukv.shape[2]
    bm = _blk(t, 256)
    row = lambda i: (i, 0)
    layer = lambda i: (l, 0, 0)
    return pl.pallas_call(
        functools.partial(_mla_proj_kernel, q_lora=q_lora, kv_lora=kv_lora),
        grid=(t // bm,),
        in_specs=[
            pl.BlockSpec((bm, d), row),
            _resident((None, d, wa_cols), layer),
            _resident((None, q_lora, nq), layer),
            _resident((None, kv_lora, nkv), layer),
            _resident((None, 1, q_lora), layer),
            _resident((None, 1, kv_lora), layer),
            pl.BlockSpec((bm, LANES), row),
            pl.BlockSpec((bm, LANES), row),
            pl.BlockSpec((bm, LANES), row),
        ],
        out_specs=[
            pl.BlockSpec((bm, nq), row),
            pl.BlockSpec((bm, nkv), row),
            pl.BlockSpec((bm, LANES), row),
        ],
        out_shape=[
            jax.ShapeDtypeStruct((t, nq), BF16),
            jax.ShapeDtypeStruct((t, nkv), BF16),
            jax.ShapeDtypeStruct((t, LANES), BF16),
        ],
        compiler_params=_params(("parallel",)),
        name="mla_proj",
    )(x_bf, w_in_p, wuq, wukv, qg, kvg, c_tab, s1_tab, s2_tab)


def _attn_kernel(q_ref, kv_ref, kr_ref, o_ref, kcat_ref, vaug_ref, *, seq, tq):
    kcat_ref[:, :LANES] = kv_ref[0, :, :LANES]
    kcat_ref[:, LANES:] = kr_ref[0]
    vaug_ref[:, :LANES] = kv_ref[0, :, LANES:]
    ones_col = lax.broadcasted_iota(jnp.int32, (seq, LANES), 1) == 0
    vaug_ref[:, LANES:] = jnp.where(ones_col, 1.0, 0.0).astype(BF16)
    rq = lax.broadcasted_iota(jnp.int32, (tq, tq), 0) // CHUNK
    ck = lax.broadcasted_iota(jnp.int32, (tq, tq), 1) // CHUNK
    diag_ok = ck <= rq
    expo = MLA_SCALE * LOG2_E
    for i in range(seq // tq):
        lo = i * tq
        q = q_ref[0, lo:lo + tq, :]
        s_d = jnp.where(diag_ok, _dot_nt(q, kcat_ref[lo:lo + tq, :]), NEG_BIG)
        m = jnp.max(s_d, -1, keepdims=True)
        if i > 0:
            s_f = _dot_nt(q, kcat_ref[:lo, :])
            m = jnp.maximum(m, jnp.max(s_f, -1, keepdims=True))
        o = _dot(jnp.exp2((s_d - m) * expo).astype(BF16), vaug_ref[lo:lo + tq, :])
        if i > 0:
            o = o + _dot(jnp.exp2((s_f - m) * expo).astype(BF16), vaug_ref[:lo, :])
        o_ref[0, lo:lo + tq, :] = (o[:, :LANES] / o[:, LANES:LANES + 1]).astype(BF16)


def _attention(q, kv, kr, batch, seq):
    tq = _blk(seq, 256)
    return pl.pallas_call(
        functools.partial(_attn_kernel, seq=seq, tq=tq),
        grid=(batch, MLA_HEADS),
        in_specs=[
            pl.BlockSpec((1, seq, Q_HEAD_PAD), lambda b, h: (b, 0, h)),
            pl.BlockSpec((1, seq, QK_NOPE + V_DIM), lambda b, h: (b, 0, h)),
            pl.BlockSpec((1, seq, LANES), lambda b, h: (b, 0, 0)),
        ],
        out_specs=pl.BlockSpec((1, seq, V_DIM), lambda b, h: (b, 0, h)),
        out_shape=jax.ShapeDtypeStruct((batch, seq, MLA_HEADS * V_DIM), BF16),
        scratch_shapes=[pltpu.VMEM((seq, Q_HEAD_PAD), BF16), pltpu.VMEM((seq, V_DIM + LANES), BF16)],
        compiler_params=_params(("parallel", "parallel")),
        name="mla_attention",
    )(q, kv, kr)


def _xproj_kernel(x_ref, wa_ref, wg_ref, wm_ref, u_ref, mq_ref):
    x = x_ref[...]
    u_ref[...] = _dot(x, wa_ref[...]) * jax.nn.sigmoid(_dot(x, wg_ref[...]))
    mq_ref[...] = _dot(x, wm_ref[...]).astype(BF16)


def _xproj(l, x_bf, w_in_p, col_a, col_g, col_m, n):
    t, d = x_bf.shape
    bm, bn = _blk(t, 512), _blk(n, 512)
    wspec = lambda col: pl.BlockSpec((None, d, bn), lambda j, i: (l, 0, col // bn + j))
    ospec = pl.BlockSpec((bm, bn), lambda j, i: (i, j))
    return pl.pallas_call(
        _xproj_kernel,
        grid=(n // bn, t // bm),
        in_specs=[pl.BlockSpec((bm, d), lambda j, i: (i, 0)), wspec(col_a), wspec(col_g), wspec(col_m)],
        out_specs=[ospec, ospec],
        out_shape=[jax.ShapeDtypeStruct((t, n), F32), jax.ShapeDtypeStruct((t, n), BF16)],
        compiler_params=_params(("parallel", "parallel")),
        name="xproj_glu_memq",
    )(x_bf, w_in_p, w_in_p, w_in_p)


def _mm_kernel(a_ref, w_ref, o_ref):
    o_ref[...] = _dot(a_ref[...], w_ref[...]).astype(o_ref.dtype)


def _matmul_all_layers(a, w, out_dtype):
    m, k = a.shape
    depth, _, n = w.shape
    bm, bn = _blk(m, 512), _blk(n, 512)
    return pl.pallas_call(
        _mm_kernel,
        grid=(depth, n // bn, m // bm),
        in_specs=[pl.BlockSpec((bm, k), lambda l, j, i: (i, 0)),
                  pl.BlockSpec((None, k, bn), lambda l, j, i: (l, 0, j))],
        out_specs=pl.BlockSpec((None, bm, bn), lambda l, j, i: (l, i, j)),
        out_shape=jax.ShapeDtypeStruct((depth, m, n), out_dtype),
        compiler_params=_params(("parallel", "parallel", "parallel")),
        name="matmul_all_layers",
    )(a, w)


def _conv_kernel(cur_ref, prev_ref, w_ref, b_ref, g_ref, beta_ref, o_ref, win_ref, *, ts, rt):
    i = pl.program_id(1)
    win_ref[0:CONV_HALO, :] = jnp.where(i > 0, prev_ref[0, ts - CONV_HALO:ts, :], 0.0)
    win_ref[CONV_HALO:CONV_HALO + ts, :] = cur_ref[0]
    win_ref[CONV_HALO + ts:, :] = jnp.zeros((SUBLANES, win_ref.shape[1]), F32)
    first_tap = CONV_HALO - (CONV_WIDTH - 1)
    for r in range(ts // rt):
        r0 = r * rt
        acc = None
        for phase in range(SUBLANES):
            ys = None
            for k in range(CONV_WIDTH):
                off = first_tap + k
                if off % SUBLANES != phase:
                    continue
                base = r0 + off - phase
                term = win_ref[base:base + rt + SUBLANES, :] * w_ref[k:k + 1, :]
                ys = term if ys is None else ys + term
            part = ys[phase:phase + rt, :]
            acc = part if acc is None else acc + part
        v = _layer_norm_rows(acc + b_ref[...], g_ref[...], beta_ref[...])
        o_ref[0, r0:r0 + rt, :] = (v * jax.nn.sigmoid(v)).astype(BF16)


def _conv_module(l, u0, conv_w, conv_b, ln_g, ln_b, batch, seq):
    c = u0.shape[-1]
    ts = _blk(seq, 256)
    assert ts >= CONV_HALO
    rt = _blk(ts, 64)
    vec = lambda b, i: (l, 0, 0)
    return pl.pallas_call(
        functools.partial(_conv_kernel, ts=ts, rt=rt),
        grid=(batch, seq // ts),
        in_specs=[
            pl.BlockSpec((1, ts, c), lambda b, i: (b, i, 0)),
            pl.BlockSpec((1, ts, c), lambda b, i: (b, jnp.maximum(i - 1, 0), 0)),
            pl.BlockSpec((None, CONV_HALO, c), vec),
            pl.BlockSpec((None, 1, c), vec),
            pl.BlockSpec((None, 1, c), vec),
            pl.BlockSpec((None, 1, c), vec),
        ],
        out_specs=pl.BlockSpec((1, ts, c), lambda b, i: (b, i, 0)),
        out_shape=jax.ShapeDtypeStruct((batch, seq, c), BF16),
        scratch_shapes=[pltpu.VMEM((CONV_HALO + ts + SUBLANES, c), F32)],
        compiler_params=_params(("parallel", "arbitrary")),
        name="conv_module",
    )(u0, u0, conv_w, conv_b, ln_g, ln_b)


def _mem_attn_kernel(q_ref, kv_ref, o_ref):
    width = MEM_HEADS * MEM_HEAD_DIM
    for h in range(MEM_HEADS):
        lo = h * MEM_HEAD_DIM
        s = _dot_nt(q_ref[0, :, lo:lo + MEM_HEAD_DIM], kv_ref[0, :, lo:lo + MEM_HEAD_DIM]) * MEM_SCALE
        p = jnp.exp(s - jnp.max(s, -1, keepdims=True))
        l = jnp.sum(p, -1, keepdims=True)
        o = _dot(p.astype(BF16), kv_ref[0, :, width + lo:width + lo + MEM_HEAD_DIM])
        o_ref[0, :, lo:lo + MEM_HEAD_DIM] = (o / l).astype(BF16)


def _mem_attention(l, mq, kvm, batch, seq):
    width = MEM_HEADS * MEM_HEAD_DIM
    m = kvm.shape[1]
    ts = _blk(seq, 512)
    return pl.pallas_call(
        _mem_attn_kernel,
        grid=(batch, seq // ts),
        in_specs=[
            pl.BlockSpec((1, ts, width), lambda b, i: (b, i, 0)),
            pl.BlockSpec((1, m, 2 * width), lambda b, i: (l * batch + b, 0, 0)),
        ],
        out_specs=pl.BlockSpec((1, ts, width), lambda b, i: (b, i, 0)),
        out_shape=jax.ShapeDtypeStruct((batch, seq, width), BF16),
        compiler_params=_params(("parallel", "parallel")),
        name="mem_attention",
    )(mq, kvm)


def _merge_kernel(x_ref, o_ref, u_ref, m_ref, wg0_ref, wg1_ref, wg2_ref, w0_ref, w1_ref, w2_ref, z_ref):
    x = x_ref[...]
    z = jax.nn.sigmoid(_dot(x, wg0_ref[...])) * _dot(o_ref[...], w0_ref[...])
    z = z + jax.nn.sigmoid(_dot(x, wg1_ref[...])) * _dot(u_ref[...], w1_ref[...])
    z = z + jax.nn.sigmoid(_dot(x, wg2_ref[...])) * _dot(m_ref[...], w2_ref[...])
    z_ref[...] = z.astype(BF16)


def _merge(l, x_bf, o_mla, u_conv, o_mem, w_in_p, col_gate, w_o_mla, w_conv_out, w_o_mem):
    t, d = x_bf.shape
    bm, bn = _blk(t, 512), _blk(d, 256)
    nj = d // bn
    act = lambda a: pl.BlockSpec((bm, a.shape[1]), lambda j, i: (i, 0))
    gate = lambda b: pl.BlockSpec((None, d, bn), lambda j, i, b=b: (l, 0, col_gate // bn + b * nj + j))
    wbr = lambda w: pl.BlockSpec((None, w.shape[1], bn), lambda j, i: (l, 0, j))
    return pl.pallas_call(
        _merge_kernel,
        grid=(nj, t // bm),
        in_specs=[act(x_bf), act(o_mla), act(u_conv), act(o_mem), gate(0), gate(1), gate(2),
                  wbr(w_o_mla), wbr(w_conv_out), wbr(w_o_mem)],
        out_specs=pl.BlockSpec((bm, bn), lambda j, i: (i, j)),
        out_shape=jax.ShapeDtypeStruct((t, d), BF16),
        compiler_params=_params(("parallel", "parallel")),
        name="gated_merge",
    )(x_bf, o_mla, u_conv, o_mem, w_in_p, w_in_p, w_in_p, w_o_mla, w_conv_out, w_o_mem)


def _route_rows(logits, b_router):
    scores = jax.nn.sigmoid(logits[:N_EXPERTS])
    biased = scores + b_router[:N_EXPERTS]
    sc = [scores[e:e + 1] for e in range(N_EXPERTS)]
    bg = [biased[e:e + 1] for e in range(N_EXPERTS)]
    group_score = []
    for g in range(N_GROUPS):
        v = bg[g * EXPERTS_PER_GROUP:(g + 1) * EXPERTS_PER_GROUP]
        best = None
        for a in range(EXPERTS_PER_GROUP):
            for b in range(a + 1, EXPERTS_PER_GROUP):
                pair = v[a] + v[b]
                best = pair if best is None else jnp.maximum(best, pair)
        group_score.append(best)
    best, best_g = group_score[0], jnp.zeros_like(group_score[0], dtype=jnp.int32)
    for g in range(1, N_GROUPS):
        better = group_score[g] > best
        best = jnp.where(better, group_score[g], best)
        best_g = jnp.where(better, g, best_g)
    sel, picked = [], []
    for e in range(N_EXPERTS):
        g, j = divmod(e, EXPERTS_PER_GROUP)
        rank = jnp.zeros_like(best_g)
        for jj in range(EXPERTS_PER_GROUP):
            if jj == j:
                continue
            other = bg[g * EXPERTS_PER_GROUP + jj]
            ahead = (other > bg[e]) | ((other == bg[e]) & (jj < j))
            rank = rank + ahead.astype(jnp.int32)
        s = (best_g == g) & (rank < TOP_K)
        sel.append(s)
        picked.append(jnp.where(s, sc[e], 0.0))
    denom = picked[0]
    for e in range(1, N_EXPERTS):
        denom = denom + picked[e]
    return [p / denom for p in picked], sel


def _pack_bf16_pair(lo, hi):
    lo_bits = lax.bitcast_convert_type(lo.astype(F32), jnp.uint32) >> 16
    hi_bits = lax.bitcast_convert_type(hi.astype(F32), jnp.uint32) & jnp.uint32(0xFFFF0000)
    return hi_bits | lo_bits


def _unpack_bf16_pair(words):
    lo = lax.bitcast_convert_type(words << 16, F32).astype(BF16)
    hi = lax.bitcast_convert_type(words & jnp.uint32(0xFFFF0000), F32).astype(BF16)
    return lo, hi


def _outproj_ln_kernel(z_ref, w_ref, x_ref, g_ref, b_ref, wr_ref, br_ref,
                       xo_ref, xpk_ref, route_ref, acc_ref, *, alpha, nj, bn):
    j = pl.program_id(1)
    acc_ref[j] = alpha * x_ref[...] + _dot(z_ref[...], w_ref[...])

    @pl.when(j == nj - 1)
    def _():
        d = nj * bn
        tot = jnp.sum(acc_ref[0], -1, keepdims=True)
        for jj in range(1, nj):
            tot = tot + jnp.sum(acc_ref[jj], -1, keepdims=True)
        mu = tot / d
        sq = None
        for jj in range(nj):
            dv = acc_ref[jj] - mu
            part = jnp.sum(dv * dv, -1, keepdims=True)
            sq = part if sq is None else sq + part
        rstd = lax.rsqrt(sq / d + LN_EPS)
        logits = None
        half = nj // 2
        for jj in range(half):
            pair = []
            for c in (jj, jj + half):
                lo = c * bn
                v = (acc_ref[c] - mu) * rstd * g_ref[:, lo:lo + bn] + b_ref[:, lo:lo + bn]
                xo_ref[:, lo:lo + bn] = v
                vb = v.astype(BF16)
                part = _dot_nt(wr_ref[:, lo:lo + bn], vb)
                logits = part if logits is None else logits + part
                pair.append(vb)
            xpk_ref[:, jj * bn:(jj + 1) * bn] = _pack_bf16_pair(pair[0], pair[1])
        cw, sel = _route_rows(logits, br_ref[...])
        for e in range(N_EXPERTS):
            route_ref[e:e + 1, :] = cw[e]
            route_ref[N_EXPERTS + e:N_EXPERTS + e + 1, :] = sel[e].astype(F32)


def _outproj_ln_route(l, z, w_out, x_res, ln_g, ln_b, w_router_t, b_router_col, alpha):
    t, d = z.shape
    bm, bn = _blk(t, 512), _blk(d // 2, 512)
    nj = d // bn
    fixed = lambda i, j: (0, 0)
    layer = lambda i, j: (l, 0, 0)
    return pl.pallas_call(
        functools.partial(_outproj_ln_kernel, alpha=alpha, nj=nj, bn=bn),
        grid=(t // bm, nj),
        in_specs=[
            pl.BlockSpec((bm, d), lambda i, j: (i, 0)),
            pl.BlockSpec((None, d, bn), lambda i, j: (l, 0, j)),
            pl.BlockSpec((bm, bn), lambda i, j: (i, j)),
            _resident((None, 1, d), layer),
            _resident((None, 1, d), layer),
            _resident((LANES, d), fixed),
            _resident((LANES, 1), fixed),
        ],
        out_specs=[
            pl.BlockSpec((bm, d), lambda i, j: (i, 0)),
            pl.BlockSpec((bm, d // 2), lambda i, j: (i, 0)),
            pl.BlockSpec((ROUTE_ROWS, bm), lambda i, j: (0, i)),
        ],
        out_shape=[
            jax.ShapeDtypeStruct((t, d), F32),
            jax.ShapeDtypeStruct((t, d // 2), jnp.uint32),
            jax.ShapeDtypeStruct((ROUTE_ROWS, t), F32),
        ],
        scratch_shapes=[pltpu.VMEM((nj, bm, bn), F32)],
        compiler_params=_params(("parallel", "arbitrary"), VMEM_LIMIT_OUTPROJ),
        name="outproj_ln_route",
    )(z, w_out, x_res, ln_g, ln_b, w_router_t, b_router_col)


def _row_copy(src_hbm, row, dst_ref, r, sem):
    return pltpu.make_async_copy(src_hbm.at[pl.ds(row, 1), :], dst_ref.at[pl.ds(r, 1), :], sem)


def _block_copy(src_hbm, dst_ref, sem):
    return pltpu.make_async_copy(src_hbm.at[pl.ds(0, dst_ref.shape[0]), :], dst_ref, sem)


FFN_BUFFERS = 3


def _ffn_kernel(be_ref, nused_ref, tok_a, tok_b, tok_c, x_hbm, wg_ref, wu_ref, wd_ref, y_ref,
                xbuf0, xbuf1, xbuf2, sems, *, rows):
    i = pl.program_id(0)
    n_used = nused_ref[0]
    bufs = (xbuf0, xbuf1, xbuf2)
    half = wg_ref.shape[0] // 2

    def gather(tok_ref, s):
        for r in range(rows):
            _row_copy(x_hbm, tok_ref[0, 0, r], bufs[s], r, sems.at[s]).start()

    @pl.when(i == 0)
    def _():
        gather(tok_a, 0)
        gather(tok_b, 1)

    @pl.when(i >= n_used)
    def _():
        y_ref[...] = jnp.zeros(y_ref.shape, y_ref.dtype)

    for s in range(FFN_BUFFERS):
        mine = i % FFN_BUFFERS == s

        @pl.when(mine & (i < n_used + 2))
        def _():
            _block_copy(x_hbm, bufs[s], sems.at[s]).wait()

        @pl.when(mine & (i < n_used))
        def _():
            gather(tok_c, (s + 2) % FFN_BUFFERS)
            x_lo, x_hi = _unpack_bf16_pair(bufs[s][...])
            g = _dot(x_lo, wg_ref[:half, :]) + _dot(x_hi, wg_ref[half:, :])
            u = _dot(x_lo, wu_ref[:half, :]) + _dot(x_hi, wu_ref[half:, :])
            h = (g * jax.nn.sigmoid(g)) * u
            y_ref[...] = _dot(h.astype(BF16), wd_ref[...])


def _expert_ffn(l, x_packed, slot_tok, block_e, n_used, w_gate, w_up, w_down, rows):
    p = slot_tok.shape[0]
    dh = x_packed.shape[1]
    d = 2 * dh
    de = w_gate.shape[3]
    nb = p // rows
    tok = slot_tok.reshape(nb, 1, rows)
    wsel = lambda i, be, nu: (l, be[jnp.minimum(i, nu[0] - 1)], 0, 0)
    tok_block = lambda f: pl.BlockSpec((1, 1, rows), lambda i, be, nu: (f(i), 0, 0), memory_space=pltpu.SMEM)
    return pl.pallas_call(
        functools.partial(_ffn_kernel, rows=rows),
        grid_spec=pltpu.PrefetchScalarGridSpec(
            num_scalar_prefetch=2,
            grid=(nb,),
            in_specs=[
                tok_block(lambda i: 0),
                tok_block(lambda i: 1),
                tok_block(lambda i: jnp.minimum(i + 2, nb - 1)),
                pl.BlockSpec(memory_space=pl.ANY),
                pl.BlockSpec((None, None, d, de), wsel),
                pl.BlockSpec((None, None, d, de), wsel),
                pl.BlockSpec((None, None, de, d), wsel),
            ],
            out_specs=pl.BlockSpec((rows, d), lambda i, be, nu: (i, 0)),
            scratch_shapes=[pltpu.VMEM((rows, dh), jnp.uint32)] * FFN_BUFFERS
            + [pltpu.SemaphoreType.DMA((FFN_BUFFERS,))],
        ),
        out_shape=jax.ShapeDtypeStruct((p, d), F32),
        compiler_params=_params(("arbitrary",)),
        name="moe_expert_ffn",
    )(block_e, n_used, tok, tok, tok, x_packed, w_gate, w_up, w_down)


def _combine_kernel(pos0_ref, pos1_ref, ys_hbm, x_ref, wts_ref, g_ref, b_ref, xo_ref, xbf_ref,
                    a0, b0, a1, b1, sems, *, alpha, rows, nb):
    i = pl.program_id(0)
    bufs = ((a0, b0), (a1, b1))

    def gather(pos_ref, s):
        for r in range(rows):
            _row_copy(ys_hbm, pos_ref[0, 0, r], bufs[s][0], r, sems.at[s]).start()
            _row_copy(ys_hbm, pos_ref[0, 1, r], bufs[s][1], r, sems.at[s]).start()

    def drain(s):
        _block_copy(ys_hbm, bufs[s][0], sems.at[s]).wait()
        _block_copy(ys_hbm, bufs[s][1], sems.at[s]).wait()

    @pl.when(i == 0)
    def _():
        gather(pos0_ref, 0)

    for s in range(2):
        @pl.when(i % 2 == s)
        def _():
            drain(s)
            gather(pos1_ref, 1 - s)
            y = wts_ref[:, 0:1] * bufs[s][0][...] + wts_ref[:, 1:2] * bufs[s][1][...]
            v = _layer_norm_rows(alpha * x_ref[...] + y, g_ref[...], b_ref[...])
            xo_ref[...] = v
            xbf_ref[...] = v.astype(BF16)

        @pl.when((i % 2 == s) & (i == nb - 1))
        def _():
            drain(1 - s)


def _combine_ln(l, ys, pos, wts, x_res, ln_g, ln_b, alpha):
    t, d = x_res.shape
    rows = _blk(t, 128)
    nb = t // rows
    layer = lambda i: (l, 0, 0)
    pos_blocks = pos.reshape(nb, rows, 2).transpose(0, 2, 1)
    return pl.pallas_call(
        functools.partial(_combine_kernel, alpha=alpha, rows=rows, nb=nb),
        grid=(nb,),
        in_specs=[
            pl.BlockSpec((1, 2, rows), lambda i: (0, 0, 0), memory_space=pltpu.SMEM),
            pl.BlockSpec((1, 2, rows), lambda i: (jnp.minimum(i + 1, nb - 1), 0, 0), memory_space=pltpu.SMEM),
            pl.BlockSpec(memory_space=pl.ANY),
            pl.BlockSpec((rows, d), lambda i: (i, 0)),
            pl.BlockSpec((rows, LANES), lambda i: (i, 0)),
            _resident((None, 1, d), layer),
            _resident((None, 1, d), layer),
        ],
        out_specs=[pl.BlockSpec((rows, d), lambda i: (i, 0)), pl.BlockSpec((rows, d), lambda i: (i, 0))],
        out_shape=[jax.ShapeDtypeStruct((t, d), F32), jax.ShapeDtypeStruct((t, d), BF16)],
        scratch_shapes=[pltpu.VMEM((rows, d), F32)] * 4 + [pltpu.SemaphoreType.DMA((2,))],
        compiler_params=_params(("arbitrary",)),
        name="moe_combine_ln",
    )(pos_blocks, pos_blocks, ys, x_res, wts, ln_g, ln_b)


def _dispatch_plan(route_t, rows):
    t = route_t.shape[1]
    cw = route_t[:N_EXPERTS].T
    sel = route_t[N_EXPERTS:].T > 0.5
    sel_i = sel.astype(jnp.int32)
    csum = jnp.cumsum(sel_i, axis=0)
    counts = csum[-1]
    padded = (counts + rows - 1) // rows * rows
    seg_end = jnp.cumsum(padded)
    dest = (seg_end - padded)[None, :] + (csum - sel_i)
    nb = t * TOP_K // rows + N_EXPERTS + FFN_BUFFERS - 1
    n_slots = nb * rows
    pos0 = jnp.min(jnp.where(sel, dest, n_slots), axis=1)
    pos1 = jnp.max(jnp.where(sel, dest, -1), axis=1)
    w0 = jnp.sum(jnp.where(sel & (dest == pos0[:, None]), cw, 0.0), axis=1)
    w1 = jnp.sum(jnp.where(sel & (dest == pos1[:, None]), cw, 0.0), axis=1)
    tok = jnp.arange(t, dtype=jnp.int32)
    slot_tok = jnp.zeros((n_slots,), jnp.int32).at[jnp.concatenate([pos0, pos1])].set(
        jnp.concatenate([tok, tok]), unique_indices=True)
    block_start = jnp.arange(nb, dtype=jnp.int32) * rows
    block_e = jnp.minimum(jnp.sum((seg_end[None, :] <= block_start[:, None]).astype(jnp.int32), axis=1),
                          N_EXPERTS - 1).astype(jnp.int32)
    n_used = (seg_end[-1] // rows).astype(jnp.int32).reshape(1)
    pos = jnp.stack([pos0, pos1], axis=1).astype(jnp.int32)
    wts = jnp.pad(jnp.stack([w0, w1], axis=1), ((0, 0), (0, LANES - TOP_K)))
    return slot_tok, block_e, n_used, pos, wts


def _rope_tables(positions):
    inv = 1.0 / (ROPE_THETA ** (jnp.arange(0, QK_ROPE, 2, dtype=F32) / QK_ROPE))
    ang = positions.astype(F32).reshape(-1, 1) * inv
    cos, sin = jnp.cos(ang), jnp.sin(ang)
    zero = jnp.zeros_like(cos)
    pad = jnp.zeros((cos.shape[0], LANES - QK_ROPE), F32)
    c_tab = jnp.concatenate([cos, cos, pad], 1)
    s1_tab = jnp.concatenate([zero, sin, pad], 1)
    s2_tab = jnp.concatenate([-sin, zero, pad], 1)
    return c_tab, s1_tab, s2_tab


def kernel(x, mem, positions, w_in, q_norm_g, kv_norm_g, w_uq, w_ukv, w_o_mla, conv_w, conv_b, conv_ln_g, conv_ln_b, w_conv_out, w_mem_kv, w_o_mem, w_out, ln1_g, ln1_b, w_router, b_router, w_gate, w_up, w_down, ln2_g, ln2_b):
    batch, seq, d = x.shape
    depth = w_in.shape[0]
    t = batch * seq
    q_lora, kv_lora, conv_dim = q_norm_g.shape[1], kv_norm_g.shape[1], conv_w.shape[2]
    mem_w = MEM_HEADS * MEM_HEAD_DIM
    assert conv_dim == mem_w
    alpha = (2 * depth) ** 0.25
    moe_rows = 256

    c_tab, s1_tab, s2_tab = _rope_tables(positions)
    mem_bf = mem.reshape(-1, d).astype(BF16)
    w_router_t = jnp.pad(w_router.T, ((0, LANES - N_EXPERTS), (0, 0))).astype(BF16)
    b_router_col = jnp.pad(b_router.astype(F32), (0, LANES - N_EXPERTS)).reshape(LANES, 1)

    o_conv = q_lora + kv_lora + QK_ROPE
    wa_cols = -(-(o_conv + LANES - QK_ROPE) // W_IN_ALIGN) * W_IN_ALIGN
    w_all = _cast_transposed(jnp.swapaxes(w_in, 1, 2), o_conv, wa_cols - o_conv)
    col_a, col_g, col_m = wa_cols, wa_cols + conv_dim, wa_cols + 2 * conv_dim
    col_gate = col_m + mem_w

    wuq = jnp.pad(w_uq.reshape(depth, q_lora, MLA_HEADS, QK_NOPE + QK_ROPE),
                  ((0, 0), (0, 0), (0, 0), (0, Q_HEAD_PAD - QK_NOPE - QK_ROPE))
                  ).reshape(depth, q_lora, -1).astype(BF16)
    wukv, w_o_mla_b, w_conv_out_b = w_ukv.astype(BF16), w_o_mla.astype(BF16), w_conv_out.astype(BF16)
    w_mem_kv_b, w_o_mem_b, w_out_b = w_mem_kv.astype(BF16), w_o_mem.astype(BF16), w_out.astype(BF16)
    w_gate_b, w_up_b, w_down_b = w_gate.astype(BF16), w_up.astype(BF16), w_down.astype(BF16)
    conv_w_p = jnp.pad(conv_w, ((0, 0), (0, CONV_HALO - CONV_WIDTH), (0, 0)))
    vec = lambda a: a[:, None, :]
    qg, kvg, conv_b3, conv_g3, conv_beta3 = vec(q_norm_g), vec(kv_norm_g), vec(conv_b), vec(conv_ln_g), vec(conv_ln_b)
    ln1_g3, ln1_b3, ln2_g3, ln2_b3 = vec(ln1_g), vec(ln1_b), vec(ln2_g), vec(ln2_b)

    kvm_all = _matmul_all_layers(mem_bf, w_mem_kv_b, BF16).reshape(depth * batch, -1, 2 * mem_w)

    xf = x.reshape(t, d)
    xb = xf.astype(BF16)
    for l in range(depth):
        q, kv, kr = _mla_proj(l, xb, w_all, wa_cols, wuq, wukv, qg, kvg, c_tab, s1_tab, s2_tab)
        o_mla = _attention(q.reshape(batch, seq, -1), kv.reshape(batch, seq, -1),
                           kr.reshape(batch, seq, LANES), batch, seq).reshape(t, -1)

        u0, mq = _xproj(l, xb, w_all, col_a, col_g, col_m, conv_dim)
        u_conv = _conv_module(l, u0.reshape(batch, seq, conv_dim), conv_w_p, conv_b3, conv_g3, conv_beta3,
                              batch, seq).reshape(t, -1)

        o_mem = _mem_attention(l, mq.reshape(batch, seq, mem_w), kvm_all, batch, seq).reshape(t, -1)

        z = _merge(l, xb, o_mla, u_conv, o_mem, w_all, col_gate, w_o_mla_b, w_conv_out_b, w_o_mem_b)
        x1, x1_packed, route_t = _outproj_ln_route(l, z, w_out_b, xf, ln1_g3, ln1_b3, w_router_t, b_router_col,
                                                   alpha)

        slot_tok, block_e, n_used, pos, wts = _dispatch_plan(route_t, moe_rows)
        ys = _expert_ffn(l, x1_packed, slot_tok, block_e, n_used, w_gate_b, w_up_b, w_down_b, moe_rows)
        xf, xb = _combine_ln(l, ys, pos, wts, x1, ln2_g3, ln2_b3, alpha)
    return xf.reshape(batch, seq, d)
```

```python
import functools

import jax
import jax.numpy as jnp
from jax import lax
from jax.experimental import pallas as pl
from jax.experimental.pallas import tpu as pltpu

F32 = jnp.float32
BF16 = jnp.bfloat16

CHUNK = 64
MLA_HEADS = 16
QK_NOPE = 128
QK_ROPE = 64
V_DIM = 128
ROPE_THETA = 10000.0
MLA_SCALE = (QK_NOPE + QK_ROPE) ** -0.5
CONV_WIDTH = 31
MEM_HEADS = 4
MEM_HEAD_DIM = 256
MEM_SCALE = MEM_HEAD_DIM ** -0.5
N_EXPERTS = 16
N_GROUPS = 4
EXPERTS_PER_GROUP = N_EXPERTS // N_GROUPS
TOP_K = 2
LN_EPS = 1e-5
RMS_EPS = 1e-6

LANES = 128
SUBLANES = 8
Q_HEAD_PAD = 2 * LANES
CONV_HALO = 32
ROUTE_ROWS = 2 * N_EXPERTS
VMEM_LIMIT = 56 * 1024 * 1024
VMEM_LIMIT_OUTPROJ = 60 * 1024 * 1024
NEG_BIG = -1e30
LOG2_E = 1.4426950408889634


def _params(dims, vmem_limit=VMEM_LIMIT):
    return pltpu.CompilerParams(dimension_semantics=dims, vmem_limit_bytes=vmem_limit)


def _blk(dim, pref):
    b = min(dim, pref)
    assert dim % b == 0, (dim, pref)
    return b


def _resident(shape, index_map):
    return pl.BlockSpec(shape, index_map, pipeline_mode=pl.Buffered(1))


def _dot(a, b):
    return jnp.dot(a, b, preferred_element_type=F32)


def _dot_nt(a, b):
    return lax.dot_general(a, b, (((1,), (1,)), ((), ())), preferred_element_type=F32)


def _layer_norm_rows(v, g, b):
    mu = jnp.mean(v, -1, keepdims=True)
    d = v - mu
    var = jnp.mean(d * d, -1, keepdims=True)
    return d * lax.rsqrt(var + LN_EPS) * g + b


CAST_PIECE = QK_ROPE
CAST_PIECES = 4
W_IN_ALIGN = 512


def _cast_t_kernel(*refs, gap_lo, gap_hi):
    o_ref = refs[-1]
    j = pl.program_id(1)
    pieces = []
    for p, r in enumerate(refs[:-1]):
        q = CAST_PIECES * j + p
        pieces.append(jnp.where((q >= gap_lo) & (q < gap_hi), 0.0, r[...]))
    w = jnp.concatenate(pieces, axis=0)
    o_ref[...] = w.T.astype(BF16)


def _cast_transposed(w_t, gap_at, gap):
    depth, n, k = w_t.shape
    width = n + gap
    bn = CAST_PIECE * CAST_PIECES
    assert gap_at % CAST_PIECE == 0 and gap % CAST_PIECE == 0 and width % bn == 0
    gap_lo, gap_hi = gap_at // CAST_PIECE, (gap_at + gap) // CAST_PIECE

    def piece(p):
        def index(l, j):
            q = CAST_PIECES * j + p
            src = jnp.where(q < gap_lo, q, jnp.where(q < gap_hi, gap_lo - 1, q - (gap_hi - gap_lo)))
            return (l, src, 0)
        return pl.BlockSpec((None, CAST_PIECE, k), index)

    return pl.pallas_call(
        functools.partial(_cast_t_kernel, gap_lo=gap_lo, gap_hi=gap_hi),
        grid=(depth, width // bn),
        in_specs=[piece(p) for p in range(CAST_PIECES)],
        out_specs=pl.BlockSpec((None, k, bn), lambda l, j: (l, 0, j)),
        out_shape=jax.ShapeDtypeStruct((depth, k, width), BF16),
        compiler_params=_params(("parallel", "parallel")),
        name="cast_transposed",
    )(*([w_t] * CAST_PIECES))


def _rope_lanes(t, c, s1, s2):
    half = QK_ROPE // 2
    return t * c + pltpu.roll(t, half, 1) * s1 + pltpu.roll(t, LANES - half, 1) * s2


def _mla_proj_kernel(x_ref, wa_ref, wuq_ref, wukv_ref, qg_ref, kvg_ref, c_ref, s1_ref, s2_ref,
                     q_ref, kv_ref, kr_ref, *, q_lora, kv_lora):
    h = _dot(x_ref[...], wa_ref[...])
    cq = h[:, :q_lora]
    ckv = h[:, q_lora:q_lora + kv_lora]
    kr = h[:, q_lora + kv_lora:q_lora + kv_lora + LANES]

    def rms(t, g):
        return t * lax.rsqrt(jnp.mean(t * t, -1, keepdims=True) + RMS_EPS) * g

    c, s1, s2 = c_ref[...], s1_ref[...], s2_ref[...]
    kr_ref[...] = _rope_lanes(kr, c, s1, s2).astype(BF16)
    kv_ref[...] = _dot(rms(ckv, kvg_ref[...]).astype(BF16), wukv_ref[...]).astype(BF16)
    q = _dot(rms(cq, qg_ref[...]).astype(BF16), wuq_ref[...])
    for hh in range(MLA_HEADS):
        lo = hh * Q_HEAD_PAD
        q_ref[:, lo:lo + LANES] = q[:, lo:lo + LANES].astype(BF16)
        q_ref[:, lo + LANES:lo + Q_HEAD_PAD] = _rope_lanes(
            q[:, lo + LANES:lo + Q_HEAD_PAD], c, s1, s2).astype(BF16)


def _mla_proj(l, x_bf, w_in_p, wa_cols, wuq, wukv, qg, kvg, c_tab, s1_tab, s2_tab):
    t, d = x_bf.shape
    q_lora, kv_lora = qg.shape[2], kvg.shape[2]
    nq, nkv = wuq.shape[2], wukv.shape[2]
    bm = _blk(t, 256)
    row = lambda i: (i, 0)
    layer = lambda i: (l, 0, 0)
    return pl.pallas_call(
        functools.partial(_mla_proj_kernel, q_lora=q_lora, kv_lora=kv_lora),
        grid=(t // bm,),
        in_specs=[
            pl.BlockSpec((bm, d), row),
            _resident((None, d, wa_cols), layer),
            _resident((None, q_lora, nq), layer),
            _resident((None, kv_lora, nkv), layer),
            _resident((None, 1, q_lora), layer),
            _resident((None, 1, kv_lora), layer),
            pl.BlockSpec((bm, LANES), row),
            pl.BlockSpec((bm, LANES), row),
            pl.BlockSpec((bm, LANES), row),
        ],
        out_specs=[
            pl.BlockSpec((bm, nq), row),
            pl.BlockSpec((bm, nkv), row),
            pl.BlockSpec((bm, LANES), row),
        ],
        out_shape=[
            jax.ShapeDtypeStruct((t, nq), BF16),
            jax.ShapeDtypeStruct((t, nkv), BF16),
            jax.ShapeDtypeStruct((t, LANES), BF16),
        ],
        compiler_params=_params(("parallel",)),
        name="mla_proj",
    )(x_bf, w_in_p, wuq, wukv, qg, kvg, c_tab, s1_tab, s2_tab)


def _attn_kernel(q_ref, kv_ref, kr_ref, o_ref, kcat_ref, vaug_ref, *, seq, tq):
    kcat_ref[:, :LANES] = kv_ref[0, :, :LANES]
    kcat_ref[:, LANES:] = kr_ref[0]
    vaug_ref[:, :LANES] = kv_ref[0, :, LANES:]
    ones_col = lax.broadcasted_iota(jnp.int32, (seq, LANES), 1) == 0
    vaug_ref[:, LANES:] = jnp.where(ones_col, 1.0, 0.0).astype(BF16)
    rq = lax.broadcasted_iota(jnp.int32, (tq, tq), 0) // CHUNK
    ck = lax.broadcasted_iota(jnp.int32, (tq, tq), 1) // CHUNK
    diag_ok = ck <= rq
    expo = MLA_SCALE * LOG2_E
    for i in range(seq // tq):
        lo = i * tq
        q = q_ref[0, lo:lo + tq, :]
        s_d = jnp.where(diag_ok, _dot_nt(q, kcat_ref[lo:lo + tq, :]), NEG_BIG)
        m = jnp.max(s_d, -1, keepdims=True)
        if i > 0:
            s_f = _dot_nt(q, kcat_ref[:lo, :])
            m = jnp.maximum(m, jnp.max(s_f, -1, keepdims=True))
        o = _dot(jnp.exp2((s_d - m) * expo).astype(BF16), vaug_ref[lo:lo + tq, :])
        if i > 0:
            o = o + _dot(jnp.exp2((s_f - m) * expo).astype(BF16), vaug_ref[:lo, :])
        o_ref[0, lo:lo + tq, :] = (o[:, :LANES] / o[:, LANES:LANES + 1]).astype(BF16)


def _attention(q, kv, kr, batch, seq):
    tq = _blk(seq, 256)
    return pl.pallas_call(
        functools.partial(_attn_kernel, seq=seq, tq=tq),
        grid=(batch, MLA_HEADS),
        in_specs=[
            pl.BlockSpec((1, seq, Q_HEAD_PAD), lambda b, h: (b, 0, h)),
            pl.BlockSpec((1, seq, QK_NOPE + V_DIM), lambda b, h: (b, 0, h)),
            pl.BlockSpec((1, seq, LANES), lambda b, h: (b, 0, 0)),
        ],
        out_specs=pl.BlockSpec((1, seq, V_DIM), lambda b, h: (b, 0, h)),
        out_shape=jax.ShapeDtypeStruct((batch, seq, MLA_HEADS * V_DIM), BF16),
        scratch_shapes=[pltpu.VMEM((seq, Q_HEAD_PAD), BF16), pltpu.VMEM((seq, V_DIM + LANES), BF16)],
        compiler_params=_params(("parallel", "parallel")),
        name="mla_attention",
    )(q, kv, kr)


def _xproj_kernel(x_ref, wa_ref, wg_ref, wm_ref, u_ref, mq_ref):
    x = x_ref[...]
    u_ref[...] = _dot(x, wa_ref[...]) * jax.nn.sigmoid(_dot(x, wg_ref[...]))
    mq_ref[...] = _dot(x, wm_ref[...]).astype(BF16)


def _xproj(l, x_bf, w_in_p, col_a, col_g, col_m, n):
    t, d = x_bf.shape
    bm, bn = _blk(t, 512), _blk(n, 512)
    wspec = lambda col: pl.BlockSpec((None, d, bn), lambda j, i: (l, 0, col // bn + j))
    ospec = pl.BlockSpec((bm, bn), lambda j, i: (i, j))
    return pl.pallas_call(
        _xproj_kernel,
        grid=(n // bn, t // bm),
        in_specs=[pl.BlockSpec((bm, d), lambda j, i: (i, 0)), wspec(col_a), wspec(col_g), wspec(col_m)],
        out_specs=[ospec, ospec],
        out_shape=[jax.ShapeDtypeStruct((t, n), F32), jax.ShapeDtypeStruct((t, n), BF16)],
        compiler_params=_params(("parallel", "parallel")),
        name="xproj_glu_memq",
    )(x_bf, w_in_p, w_in_p, w_in_p)


def _mm_kernel(a_ref, w_ref, o_ref):
    o_ref[...] = _dot(a_ref[...], w_ref[...]).astype(o_ref.dtype)


def _matmul_all_layers(a, w, out_dtype):
    m, k = a.shape
    depth, _, n = w.shape
    bm, bn = _blk(m, 512), _blk(n, 512)
    return pl.pallas_call(
        _mm_kernel,
        grid=(depth, n // bn, m // bm),
        in_specs=[pl.BlockSpec((bm, k), lambda l, j, i: (i, 0)),
                  pl.BlockSpec((None, k, bn), lambda l, j, i: (l, 0, j))],
        out_specs=pl.BlockSpec((None, bm, bn), lambda l, j, i: (l, i, j)),
        out_shape=jax.ShapeDtypeStruct((depth, m, n), out_dtype),
        compiler_params=_params(("parallel", "parallel", "parallel")),
        name="matmul_all_layers",
    )(a, w)


def _conv_kernel(cur_ref, prev_ref, w_ref, b_ref, g_ref, beta_ref, o_ref, win_ref, *, ts, rt):
    i = pl.program_id(1)
    win_ref[0:CONV_HALO, :] = jnp.where(i > 0, prev_ref[0, ts - CONV_HALO:ts, :], 0.0)
    win_ref[CONV_HALO:CONV_HALO + ts, :] = cur_ref[0]
    win_ref[CONV_HALO + ts:, :] = jnp.zeros((SUBLANES, win_ref.shape[1]), F32)
    first_tap = CONV_HALO - (CONV_WIDTH - 1)
    for r in range(ts // rt):
        r0 = r * rt
        acc = None
        for phase in range(SUBLANES):
            ys = None
            for k in range(CONV_WIDTH):
                off = first_tap + k
                if off % SUBLANES != phase:
                    continue
                base = r0 + off - phase
                term = win_ref[base:base + rt + SUBLANES, :] * w_ref[k:k + 1, :]
                ys = term if ys is None else ys + term
            part = ys[phase:phase + rt, :]
            acc = part if acc is None else acc + part
        v = _layer_norm_rows(acc + b_ref[...], g_ref[...], beta_ref[...])
        o_ref[0, r0:r0 + rt, :] = (v * jax.nn.sigmoid(v)).astype(BF16)


def _conv_module(l, u0, conv_w, conv_b, ln_g, ln_b, batch, seq):
    c = u0.shape[-1]
    ts = _blk(seq, 256)
    assert ts >= CONV_HALO
    rt = _blk(ts, 64)
    vec = lambda b, i: (l, 0, 0)
    return pl.pallas_call(
        functools.partial(_conv_kernel, ts=ts, rt=rt),
        grid=(batch, seq // ts),
        in_specs=[
            pl.BlockSpec((1, ts, c), lambda b, i: (b, i, 0)),
            pl.BlockSpec((1, ts, c), lambda b, i: (b, jnp.maximum(i - 1, 0), 0)),
            pl.BlockSpec((None, CONV_HALO, c), vec),
            pl.BlockSpec((None, 1, c), vec),
            pl.BlockSpec((None, 1, c), vec),
            pl.BlockSpec((None, 1, c), vec),
        ],
        out_specs=pl.BlockSpec((1, ts, c), lambda b, i: (b, i, 0)),
        out_shape=jax.ShapeDtypeStruct((batch, seq, c), BF16),
        scratch_shapes=[pltpu.VMEM((CONV_HALO + ts + SUBLANES, c), F32)],
        compiler_params=_params(("parallel", "arbitrary")),
        name="conv_module",
    )(u0, u0, conv_w, conv_b, ln_g, ln_b)


def _mem_attn_kernel(q_ref, kv_ref, o_ref):
    width = MEM_HEADS * MEM_HEAD_DIM
    for h in range(MEM_HEADS):
        lo = h * MEM_HEAD_DIM
        s = _dot_nt(q_ref[0, :, lo:lo + MEM_HEAD_DIM], kv_ref[0, :, lo:lo + MEM_HEAD_DIM]) * MEM_SCALE
        p = jnp.exp(s - jnp.max(s, -1, keepdims=True))
        l = jnp.sum(p, -1, keepdims=True)
        o = _dot(p.astype(BF16), kv_ref[0, :, width + lo:width + lo + MEM_HEAD_DIM])
        o_ref[0, :, lo:lo + MEM_HEAD_DIM] = (o / l).astype(BF16)


def _mem_attention(l, mq, kvm, batch, seq):
    width = MEM_HEADS * MEM_HEAD_DIM
    m = kvm.shape[1]
    ts = _blk(seq, 512)
    return pl.pallas_call(
        _mem_attn_kernel,
        grid=(batch, seq // ts),
        in_specs=[
            pl.BlockSpec((1, ts, width), lambda b, i: (b, i, 0)),
            pl.BlockSpec((1, m, 2 * width), lambda b, i: (l * batch + b, 0, 0)),
        ],
        out_specs=pl.BlockSpec((1, ts, width), lambda b, i: (b, i, 0)),
        out_shape=jax.ShapeDtypeStruct((batch, seq, width), BF16),
        compiler_params=_params(("parallel", "parallel")),
        name="mem_attention",
    )(mq, kvm)


def _merge_kernel(x_ref, o_ref, u_ref, m_ref, wg0_ref, wg1_ref, wg2_ref, w0_ref, w1_ref, w2_ref, z_ref):
    x = x_ref[...]
    z = jax.nn.sigmoid(_dot(x, wg0_ref[...])) * _dot(o_ref[...], w0_ref[...])
    z = z + jax.nn.sigmoid(_dot(x, wg1_ref[...])) * _dot(u_ref[...], w1_ref[...])
    z = z + jax.nn.sigmoid(_dot(x, wg2_ref[...])) * _dot(m_ref[...], w2_ref[...])
    z_ref[...] = z.astype(BF16)


def _merge(l, x_bf, o_mla, u_conv, o_mem, w_in_p, col_gate, w_o_mla, w_conv_out, w_o_mem):
    t, d = x_bf.shape
    bm, bn = _blk(t, 512), _blk(d, 256)
    nj = d // bn
    act = lambda a: pl.BlockSpec((bm, a.shape[1]), lambda j, i: (i, 0))
    gate = lambda b: pl.BlockSpec((None, d, bn), lambda j, i, b=b: (l, 0, col_gate // bn + b * nj + j))
    wbr = lambda w: pl.BlockSpec((None, w.shape[1], bn), lambda j, i: (l, 0, j))
    return pl.pallas_call(
        _merge_kernel,
        grid=(nj, t // bm),
        in_specs=[act(x_bf), act(o_mla), act(u_conv), act(o_mem), gate(0), gate(1), gate(2),
                  wbr(w_o_mla), wbr(w_conv_out), wbr(w_o_mem)],
        out_specs=pl.BlockSpec((bm, bn), lambda j, i: (i, j)),
        out_shape=jax.ShapeDtypeStruct((t, d), BF16),
        compiler_params=_params(("parallel", "parallel")),
        name="gated_merge",
    )(x_bf, o_mla, u_conv, o_mem, w_in_p, w_in_p, w_in_p, w_o_mla, w_conv_out, w_o_mem)


def _route_rows(logits, b_router):
    scores = jax.nn.sigmoid(logits[:N_EXPERTS])
    biased = scores + b_router[:N_EXPERTS]
    sc = [scores[e:e + 1] for e in range(N_EXPERTS)]
    bg = [biased[e:e + 1] for e in range(N_EXPERTS)]
    group_score = []
    for g in range(N_GROUPS):
        v = bg[g * EXPERTS_PER_GROUP:(g + 1) * EXPERTS_PER_GROUP]
        best = None
        for a in range(EXPERTS_PER_GROUP):
            for b in range(a + 1, EXPERTS_PER_GROUP):
                pair = v[a] + v[b]
                best = pair if best is None else jnp.maximum(best, pair)
        group_score.append(best)
    best, best_g = group_score[0], jnp.zeros_like(group_score[0], dtype=jnp.int32)
    for g in range(1, N_GROUPS):
        better = group_score[g] > best
        best = jnp.where(better, group_score[g], best)
        best_g = jnp.where(better, g, best_g)
    sel, picked = [], []
    for e in range(N_EXPERTS):
        g, j = divmod(e, EXPERTS_PER_GROUP)
        rank = jnp.zeros_like(best_g)
        for jj in range(EXPERTS_PER_GROUP):
            if jj == j:
                continue
            other = bg[g * EXPERTS_PER_GROUP + jj]
            ahead = (other > bg[e]) | ((other == bg[e]) & (jj < j))
            rank = rank + ahead.astype(jnp.int32)
        s = (best_g == g) & (rank < TOP_K)
        sel.append(s)
        picked.append(jnp.where(s, sc[e], 0.0))
    denom = picked[0]
    for e in range(1, N_EXPERTS):
        denom = denom + picked[e]
    return [p / denom for p in picked], sel


def _pack_bf16_pair(lo, hi):
    lo_bits = lax.bitcast_convert_type(lo.astype(F32), jnp.uint32) >> 16
    hi_bits = lax.bitcast_convert_type(hi.astype(F32), jnp.uint32) & jnp.uint32(0xFFFF0000)
    return hi_bits | lo_bits


def _unpack_bf16_pair(words):
    lo = lax.bitcast_convert_type(words << 16, F32).astype(BF16)
    hi = lax.bitcast_convert_type(words & jnp.uint32(0xFFFF0000), F32).astype(BF16)
    return lo, hi


def _outproj_ln_kernel(z_ref, w_ref, x_ref, g_ref, b_ref, wr_ref, br_ref,
                       xo_ref, xpk_ref, route_ref, acc_ref, *, alpha, nj, bn):
    j = pl.program_id(1)
    acc_ref[j] = alpha * x_ref[...] + _dot(z_ref[...], w_ref[...])

    @pl.when(j == nj - 1)
    def _():
        d = nj * bn
        tot = jnp.sum(acc_ref[0], -1, keepdims=True)
        for jj in range(1, nj):
            tot = tot + jnp.sum(acc_ref[jj], -1, keepdims=True)
        mu = tot / d
        sq = None
        for jj in range(nj):
            dv = acc_ref[jj] - mu
            part = jnp.sum(dv * dv, -1, keepdims=True)
            sq = part if sq is None else sq + part
        rstd = lax.rsqrt(sq / d + LN_EPS)
        logits = None
        half = nj // 2
        for jj in range(half):
            pair = []
            for c in (jj, jj + half):
                lo = c * bn
                v = (acc_ref[c] - mu) * rstd * g_ref[:, lo:lo + bn] + b_ref[:, lo:lo + bn]
                xo_ref[:, lo:lo + bn] = v
                vb = v.astype(BF16)
                part = _dot_nt(wr_ref[:, lo:lo + bn], vb)
                logits = part if logits is None else logits + part
                pair.append(vb)
            xpk_ref[:, jj * bn:(jj + 1) * bn] = _pack_bf16_pair(pair[0], pair[1])
        cw, sel = _route_rows(logits, br_ref[...])
        for e in range(N_EXPERTS):
            route_ref[e:e + 1, :] = cw[e]
            route_ref[N_EXPERTS + e:N_EXPERTS + e + 1, :] = sel[e].astype(F32)


def _outproj_ln_route(l, z, w_out, x_res, ln_g, ln_b, w_router_t, b_router_col, alpha):
    t, d = z.shape
    bm, bn = _blk(t, 512), _blk(d // 2, 512)
    nj = d // bn
    fixed = lambda i, j: (0, 0)
    layer = lambda i, j: (l, 0, 0)
    return pl.pallas_call(
        functools.partial(_outproj_ln_kernel, alpha=alpha, nj=nj, bn=bn),
        grid=(t // bm, nj),
        in_specs=[
            pl.BlockSpec((bm, d), lambda i, j: (i, 0)),
            pl.BlockSpec((None, d, bn), lambda i, j: (l, 0, j)),
            pl.BlockSpec((bm, bn), lambda i, j: (i, j)),
            _resident((None, 1, d), layer),
            _resident((None, 1, d), layer),
            _resident((LANES, d), fixed),
            _resident((LANES, 1), fixed),
        ],
        out_specs=[
            pl.BlockSpec((bm, d), lambda i, j: (i, 0)),
            pl.BlockSpec((bm, d // 2), lambda i, j: (i, 0)),
            pl.BlockSpec((ROUTE_ROWS, bm), lambda i, j: (0, i)),
        ],
        out_shape=[
            jax.ShapeDtypeStruct((t, d), F32),
            jax.ShapeDtypeStruct((t, d // 2), jnp.uint32),
            jax.ShapeDtypeStruct((ROUTE_ROWS, t), F32),
        ],
        scratch_shapes=[pltpu.VMEM((nj, bm, bn), F32)],
        compiler_params=_params(("parallel", "arbitrary"), VMEM_LIMIT_OUTPROJ),
        name="outproj_ln_route",
    )(z, w_out, x_res, ln_g, ln_b, w_router_t, b_router_col)


def _row_copy(src_hbm, row, dst_ref, r, sem):
    return pltpu.make_async_copy(src_hbm.at[pl.ds(row, 1), :], dst_ref.at[pl.ds(r, 1), :], sem)


def _block_copy(src_hbm, dst_ref, sem):
    return pltpu.make_async_copy(src_hbm.at[pl.ds(0, dst_ref.shape[0]), :], dst_ref, sem)


FFN_BUFFERS = 3


def _ffn_kernel(be_ref, nused_ref, tok_a, tok_b, tok_c, x_hbm, wg_ref, wu_ref, wd_ref, y_ref,
                xbuf0, xbuf1, xbuf2, sems, *, rows):
    i = pl.program_id(0)
    n_used = nused_ref[0]
    bufs = (xbuf0, xbuf1, xbuf2)
    half = wg_ref.shape[0] // 2

    def gather(tok_ref, s):
        for r in range(rows):
            _row_copy(x_hbm, tok_ref[0, 0, r], bufs[s], r, sems.at[s]).start()

    @pl.when(i == 0)
    def _():
        gather(tok_a, 0)
        gather(tok_b, 1)

    @pl.when(i >= n_used)
    def _():
        y_ref[...] = jnp.zeros(y_ref.shape, y_ref.dtype)

    for s in range(FFN_BUFFERS):
        mine = i % FFN_BUFFERS == s

        @pl.when(mine & (i < n_used + 2))
        def _():
            _block_copy(x_hbm, bufs[s], sems.at[s]).wait()

        @pl.when(mine & (i < n_used))
        def _():
            gather(tok_c, (s + 2) % FFN_BUFFERS)
            x_lo, x_hi = _unpack_bf16_pair(bufs[s][...])
            g = _dot(x_lo, wg_ref[:half, :]) + _dot(x_hi, wg_ref[half:, :])
            u = _dot(x_lo, wu_ref[:half, :]) + _dot(x_hi, wu_ref[half:, :])
            h = (g * jax.nn.sigmoid(g)) * u
            y_ref[...] = _dot(h.astype(BF16), wd_ref[...])


def _expert_ffn(l, x_packed, slot_tok, block_e, n_used, w_gate, w_up, w_down, rows):
    p = slot_tok.shape[0]
    dh = x_packed.shape[1]
    d = 2 * dh
    de = w_gate.shape[3]
    nb = p // rows
    tok = slot_tok.reshape(nb, 1, rows)
    wsel = lambda i, be, nu: (l, be[jnp.minimum(i, nu[0] - 1)], 0, 0)
    tok_block = lambda f: pl.BlockSpec((1, 1, rows), lambda i, be, nu: (f(i), 0, 0), memory_space=pltpu.SMEM)
    return pl.pallas_call(
        functools.partial(_ffn_kernel, rows=rows),
        grid_spec=pltpu.PrefetchScalarGridSpec(
            num_scalar_prefetch=2,
            grid=(nb,),
            in_specs=[
                tok_block(lambda i: 0),
                tok_block(lambda i: 1),
                tok_block(lambda i: jnp.minimum(i + 2, nb - 1)),
                pl.BlockSpec(memory_space=pl.ANY),
                pl.BlockSpec((None, None, d, de), wsel),
                pl.BlockSpec((None, None, d, de), wsel),
                pl.BlockSpec((None, None, de, d), wsel),
            ],
            out_specs=pl.BlockSpec((rows, d), lambda i, be, nu: (i, 0)),
            scratch_shapes=[pltpu.VMEM((rows, dh), jnp.uint32)] * FFN_BUFFERS
            + [pltpu.SemaphoreType.DMA((FFN_BUFFERS,))],
        ),
        out_shape=jax.ShapeDtypeStruct((p, d), F32),
        compiler_params=_params(("arbitrary",)),
        name="moe_expert_ffn",
    )(block_e, n_used, tok, tok, tok, x_packed, w_gate, w_up, w_down)


def _combine_kernel(pos0_ref, pos1_ref, ys_hbm, x_ref, wts_ref, g_ref, b_ref, xo_ref, xbf_ref,
                    a0, b0, a1, b1, sems, *, alpha, rows, nb):
    i = pl.program_id(0)
    bufs = ((a0, b0), (a1, b1))

    def gather(pos_ref, s):
        for r in range(rows):
            _row_copy(ys_hbm, pos_ref[0, 0, r], bufs[s][0], r, sems.at[s]).start()
            _row_copy(ys_hbm, pos_ref[0, 1, r], bufs[s][1], r, sems.at[s]).start()

    def drain(s):
        _block_copy(ys_hbm, bufs[s][0], sems.at[s]).wait()
        _block_copy(ys_hbm, bufs[s][1], sems.at[s]).wait()

    @pl.when(i == 0)
    def _():
        gather(pos0_ref, 0)

    for s in range(2):
        @pl.when(i % 2 == s)
        def _():
            drain(s)
            gather(pos1_ref, 1 - s)
            y = wts_ref[:, 0:1] * bufs[s][0][...] + wts_ref[:, 1:2] * bufs[s][1][...]
            v = _layer_norm_rows(alpha * x_ref[...] + y, g_ref[...], b_ref[...])
            xo_ref[...] = v
            xbf_ref[...] = v.astype(BF16)

        @pl.when((i % 2 == s) & (i == nb - 1))
        def _():
            drain(1 - s)


def _combine_ln(l, ys, pos, wts, x_res, ln_g, ln_b, alpha):
    t, d = x_res.shape
    rows = _blk(t, 128)
    nb = t // rows
    layer = lambda i: (l, 0, 0)
    pos_blocks = pos.reshape(nb, rows, 2).transpose(0, 2, 1)
    return pl.pallas_call(
        functools.partial(_combine_kernel, alpha=alpha, rows=rows, nb=nb),
        grid=(nb,),
        in_specs=[
            pl.BlockSpec((1, 2, rows), lambda i: (0, 0, 0), memory_space=pltpu.SMEM),
            pl.BlockSpec((1, 2, rows), lambda i: (jnp.minimum(i + 1, nb - 1), 0, 0), memory_space=pltpu.SMEM),
            pl.BlockSpec(memory_space=pl.ANY),
            pl.BlockSpec((rows, d), lambda i: (i, 0)),
            pl.BlockSpec((rows, LANES), lambda i: (i, 0)),
            _resident((None, 1, d), layer),
            _resident((None, 1, d), layer),
        ],
        out_specs=[pl.BlockSpec((rows, d), lambda i: (i, 0)), pl.BlockSpec((rows, d), lambda i: (i, 0))],
        out_shape=[jax.ShapeDtypeStruct((t, d), F32), jax.ShapeDtypeStruct((t, d), BF16)],
        scratch_shapes=[pltpu.VMEM((rows, d), F32)] * 4 + [pltpu.SemaphoreType.DMA((2,))],
        compiler_params=_params(("arbitrary",)),
        name="moe_combine_ln",
    )(pos_blocks, pos_blocks, ys, x_res, wts, ln_g, ln_b)


def _dispatch_plan(route_t, rows):
    t = route_t.shape[1]
    cw = route_t[:N_EXPERTS].T
    sel = route_t[N_EXPERTS:].T > 0.5
    sel_i = sel.astype(jnp.int32)
    csum = jnp.cumsum(sel_i, axis=0)
    counts = csum[-1]
    padded = (counts + rows - 1) // rows * rows
    seg_end = jnp.cumsum(padded)
    dest = (seg_end - padded)[None, :] + (csum - sel_i)
    nb = t * TOP_K // rows + N_EXPERTS + FFN_BUFFERS - 1
    n_slots = nb * rows
    pos0 = jnp.min(jnp.where(sel, dest, n_slots), axis=1)
    pos1 = jnp.max(jnp.where(sel, dest, -1), axis=1)
    w0 = jnp.sum(jnp.where(sel & (dest == pos0[:, None]), cw, 0.0), axis=1)
    w1 = jnp.sum(jnp.where(sel & (dest == pos1[:, None]), cw, 0.0), axis=1)
    tok = jnp.arange(t, dtype=jnp.int32)
    slot_tok = jnp.zeros((n_slots,), jnp.int32).at[jnp.concatenate([pos0, pos1])].set(
        jnp.concatenate([tok, tok]), unique_indices=True)
    block_start = jnp.arange(nb, dtype=jnp.int32) * rows
    block_e = jnp.minimum(jnp.sum((seg_end[None, :] <= block_start[:, None]).astype(jnp.int32), axis=1),
                          N_EXPERTS - 1).astype(jnp.int32)
    n_used = (seg_end[-1] // rows).astype(jnp.int32).reshape(1)
    pos = jnp.stack([pos0, pos1], axis=1).astype(jnp.int32)
    wts = jnp.pad(jnp.stack([w0, w1], axis=1), ((0, 0), (0, LANES - TOP_K)))
    return slot_tok, block_e, n_used, pos, wts


def _rope_tables(positions):
    inv = 1.0 / (ROPE_THETA ** (jnp.arange(0, QK_ROPE, 2, dtype=F32) / QK_ROPE))
    ang = positions.astype(F32).reshape(-1, 1) * inv
    cos, sin = jnp.cos(ang), jnp.sin(ang)
    zero = jnp.zeros_like(cos)
    pad = jnp.zeros((cos.shape[0], LANES - QK_ROPE), F32)
    c_tab = jnp.concatenate([cos, cos, pad], 1)
    s1_tab = jnp.concatenate([zero, sin, pad], 1)
    s2_tab = jnp.concatenate([-sin, zero, pad], 1)
    return c_tab, s1_tab, s2_tab


def kernel(x, mem, positions, w_in, q_norm_g, kv_norm_g, w_uq, w_ukv, w_o_mla, conv_w, conv_b, conv_ln_g, conv_ln_b, w_conv_out, w_mem_kv, w_o_mem, w_out, ln1_g, ln1_b, w_router, b_router, w_gate, w_up, w_down, ln2_g, ln2_b):
    batch, seq, d = x.shape
    depth = w_in.shape[0]
    t = batch * seq
    q_lora, kv_lora, conv_dim = q_norm_g.shape[1], kv_norm_g.shape[1], conv_w.shape[2]
    mem_w = MEM_HEADS * MEM_HEAD_DIM
    assert conv_dim == mem_w
    alpha = (2 * depth) ** 0.25
    moe_rows = 256

    c_tab, s1_tab, s2_tab = _rope_tables(positions)
    mem_bf = mem.reshape(-1, d).astype(BF16)
    w_router_t = jnp.pad(w_router.T, ((0, LANES - N_EXPERTS), (0, 0))).astype(BF16)
    b_router_col = jnp.pad(b_router.astype(F32), (0, LANES - N_EXPERTS)).reshape(LANES, 1)

    o_conv = q_lora + kv_lora + QK_ROPE
    wa_cols = -(-(o_conv + LANES - QK_ROPE) // W_IN_ALIGN) * W_IN_ALIGN
    w_all = _cast_transposed(jnp.swapaxes(w_in, 1, 2), o_conv, wa_cols - o_conv)
    col_a, col_g, col_m = wa_cols, wa_cols + conv_dim, wa_cols + 2 * conv_dim
    col_gate = col_m + mem_w

    wuq = jnp.pad(w_uq.reshape(depth, q_lora, MLA_HEADS, QK_NOPE + QK_ROPE),
                  ((0, 0), (0, 0), (0, 0), (0, Q_HEAD_PAD - QK_NOPE - QK_ROPE))
                  ).reshape(depth, q_lora, -1).astype(BF16)
    wukv, w_o_mla_b, w_conv_out_b = w_ukv.astype(BF16), w_o_mla.astype(BF16), w_conv_out.astype(BF16)
    w_mem_kv_b, w_o_mem_b, w_out_b = w_mem_kv.astype(BF16), w_o_mem.astype(BF16), w_out.astype(BF16)
    w_gate_b, w_up_b, w_down_b = w_gate.astype(BF16), w_up.astype(BF16), w_down.astype(BF16)
    conv_w_p = jnp.pad(conv_w, ((0, 0), (0, CONV_HALO - CONV_WIDTH), (0, 0)))
    vec = lambda a: a[:, None, :]
    qg, kvg, conv_b3, conv_g3, conv_beta3 = vec(q_norm_g), vec(kv_norm_g), vec(conv_b), vec(conv_ln_g), vec(conv_ln_b)
    ln1_g3, ln1_b3, ln2_g3, ln2_b3 = vec(ln1_g), vec(ln1_b), vec(ln2_g), vec(ln2_b)

    kvm_all = _matmul_all_layers(mem_bf, w_mem_kv_b, BF16).reshape(depth * batch, -1, 2 * mem_w)

    xf = x.reshape(t, d)
    xb = xf.astype(BF16)
    for l in range(depth):
        q, kv, kr = _mla_proj(l, xb, w_all, o_conv + LANES - QK_ROPE, wuq, wukv, qg, kvg, c_tab, s1_tab, s2_tab)
        o_mla = _attention(q.reshape(batch, seq, -1), kv.reshape(batch, seq, -1),
                           kr.reshape(batch, seq, LANES), batch, seq).reshape(t, -1)

        u0, mq = _xproj(l, xb, w_all, col_a, col_g, col_m, conv_dim)
        u_conv = _conv_module(l, u0.reshape(batch, seq, conv_dim), conv_w_p, conv_b3, conv_g3, conv_beta3,
                              batch, seq).reshape(t, -1)

        o_mem = _mem_attention(l, mq.reshape(batch, seq, mem_w), kvm_all, batch, seq).reshape(t, -1)

        z = _merge(l, xb, o_mla, u_conv, o_mem, w_all, col_gate, w_o_mla_b, w_conv_out_b, w_o_mem_b)
        x1, x1_packed, route_t = _outproj_ln_route(l, z, w_out_b, xf, ln1_g3, ln1_b3, w_router_t, b_router_col,
                                                   alpha)

        slot_tok, block_e, n_used, pos, wts = _dispatch_plan(route_t, moe_rows)
        ys = _expert_ffn(l, x1_packed, slot_tok, block_e, n_used, w_gate_b, w_up_b, w_down_b, moe_rows)
        xf, xb = _combine_ln(l, ys, pos, wts, x1, ln2_g3, ln2_b3, alpha)
    return xf.reshape(batch, seq, d)
```

```python
import functools

import jax
import jax.numpy as jnp
from jax import lax
from jax.experimental import pallas as pl
from jax.experimental.pallas import tpu as pltpu

F32 = jnp.float32
BF16 = jnp.bfloat16

CHUNK = 64
MLA_HEADS = 16
QK_NOPE = 128
QK_ROPE = 64
V_DIM = 128
ROPE_THETA = 10000.0
MLA_SCALE = (QK_NOPE + QK_ROPE) ** -0.5
CONV_WIDTH = 31
MEM_HEADS = 4
MEM_HEAD_DIM = 256
MEM_SCALE = MEM_HEAD_DIM ** -0.5
N_EXPERTS = 16
N_GROUPS = 4
EXPERTS_PER_GROUP = N_EXPERTS // N_GROUPS
TOP_K = 2
LN_EPS = 1e-5
RMS_EPS = 1e-6

LANES = 128
SUBLANES = 8
Q_HEAD_PAD = 2 * LANES
CONV_HALO = 32
ROUTE_ROWS = 2 * N_EXPERTS
VMEM_LIMIT = 56 * 1024 * 1024
VMEM_LIMIT_OUTPROJ = 60 * 1024 * 1024
NEG_BIG = -1e30
LOG2_E = 1.4426950408889634


def _params(dims, vmem_limit=VMEM_LIMIT):
    return pltpu.CompilerParams(dimension_semantics=dims, vmem_limit_bytes=vmem_limit)


def _blk(dim, pref):
    b = min(dim, pref)
    assert dim % b == 0, (dim, pref)
    return b


def _resident(shape, index_map):
    return pl.BlockSpec(shape, index_map, pipeline_mode=pl.Buffered(1))


def _dot(a, b):
    return jnp.dot(a, b, preferred_element_type=F32)


def _dot_nt(a, b):
    return lax.dot_general(a, b, (((1,), (1,)), ((), ())), preferred_element_type=F32)


def _layer_norm_rows(v, g, b):
    mu = jnp.mean(v, -1, keepdims=True)
    d = v - mu
    var = jnp.mean(d * d, -1, keepdims=True)
    return d * lax.rsqrt(var + LN_EPS) * g + b


CAST_PIECE = QK_ROPE
CAST_PIECES = 4
W_IN_ALIGN = 512


def _cast_t_kernel(*refs, gap_lo, gap_hi):
    o_ref = refs[-1]
    j = pl.program_id(1)
    pieces = []
    for p, r in enumerate(refs[:-1]):
        q = CAST_PIECES * j + p
        pieces.append(jnp.where((q >= gap_lo) & (q < gap_hi), 0.0, r[...]))
    w = jnp.concatenate(pieces, axis=0)
    o_ref[...] = w.T.astype(BF16)


def _cast_transposed(w_t, gap_at, gap):
    depth, n, k = w_t.shape
    width = n + gap
    bn = CAST_PIECE * CAST_PIECES
    assert gap_at % CAST_PIECE == 0 and gap % CAST_PIECE == 0 and width % bn == 0
    gap_lo, gap_hi = gap_at // CAST_PIECE, (gap_at + gap) // CAST_PIECE

    def piece(p):
        def index(l, j):
            q = CAST_PIECES * j + p
            src = jnp.where(q < gap_lo, q, jnp.where(q < gap_hi, gap_lo - 1, q - (gap_hi - gap_lo)))
            return (l, src, 0)
        return pl.BlockSpec((None, CAST_PIECE, k), index)

    return pl.pallas_call(
        functools.partial(_cast_t_kernel, gap_lo=gap_lo, gap_hi=gap_hi),
        grid=(depth, width // bn),
        in_specs=[piece(p) for p in range(CAST_PIECES)],
        out_specs=pl.BlockSpec((None, k, bn), lambda l, j: (l, 0, j)),
        out_shape=jax.ShapeDtypeStruct((depth, k, width), BF16),
        compiler_params=_params(("parallel", "parallel")),
        name="cast_transposed",
    )(*([w_t] * CAST_PIECES))


def _rope_lanes(t, c, s1, s2):
    half = QK_ROPE // 2
    return t * c + pltpu.roll(t, half, 1) * s1 + pltpu.roll(t, LANES - half, 1) * s2


def _mla_proj_kernel(x_ref, wa_ref, wuq_ref, wukv_ref, qg_ref, kvg_ref, c_ref, s1_ref, s2_ref,
                     q_ref, kv_ref, kr_ref, *, q_lora, kv_lora):
    h = _dot(x_ref[...], wa_ref[...])
    cq = h[:, :q_lora]
    ckv = h[:, q_lora:q_lora + kv_lora]
    kr = h[:, q_lora + kv_lora:q_lora + kv_lora + LANES]

    def rms(t, g):
        return t * lax.rsqrt(jnp.mean(t * t, -1, keepdims=True) + RMS_EPS) * g

    c, s1, s2 = c_ref[...], s1_ref[...], s2_ref[...]
    kr_ref[...] = _rope_lanes(kr, c, s1, s2).astype(BF16)
    kv_ref[...] = _dot(rms(ckv, kvg_ref[...]).astype(BF16), wukv_ref[...]).astype(BF16)
    q = _dot(rms(cq, qg_ref[...]).astype(BF16), wuq_ref[...])
    for hh in range(MLA_HEADS):
        lo = hh * Q_HEAD_PAD
        q_ref[:, lo:lo + LANES] = q[:, lo:lo + LANES].astype(BF16)
        q_ref[:, lo + LANES:lo + Q_HEAD_PAD] = _rope_lanes(
            q[:, lo + LANES:lo + Q_HEAD_PAD], c, s1, s2).astype(BF16)


def _mla_proj(l, x_bf, w_in_p, wa_cols, wuq, wukv, qg, kvg, c_tab, s1_tab, s2_tab):
    t, d = x_bf.shape
    q_lora, kv_lora = qg.shape[2], kvg.shape[2]
    nq, nkv = wuq.shape[2], wukv.shape[2]
    bm = _blk(t, 256)
    row = lambda i: (i, 0)
    layer = lambda i: (l, 0, 0)
    return pl.pallas_call(
        functools.partial(_mla_proj_kernel, q_lora=q_lora, kv_lora=kv_lora),
        grid=(t // bm,),
        in_specs=[
            pl.BlockSpec((bm, d), row),
            _resident((None, d, wa_cols), layer),
            _resident((None, q_lora, nq), layer),
            _resident((None, kv_lora, nkv), layer),
            _resident((None, 1, q_lora), layer),
            _resident((None, 1, kv_lora), layer),
            pl.BlockSpec((bm, LANES), row),
            pl.BlockSpec((bm, LANES), row),
            pl.BlockSpec((bm, LANES), row),
        ],
        out_specs=[
            pl.BlockSpec((bm, nq), row),
            pl.BlockSpec((bm, nkv), row),
            pl.BlockSpec((bm, LANES), row),
        ],
        out_shape=[
            jax.ShapeDtypeStruct((t, nq), BF16),
            jax.ShapeDtypeStruct((t, nkv), BF16),
            jax.ShapeDtypeStruct((t, LANES), BF16),
        ],
        compiler_params=_params(("parallel",)),
        name="mla_proj",
    )(x_bf, w_in_p, wuq, wukv, qg, kvg, c_tab, s1_tab, s2_tab)


def _attn_kernel(q_ref, kv_ref, kr_ref, o_ref, kcat_ref, vaug_ref, *, seq, tq):
    kcat_ref[:, :LANES] = kv_ref[0, :, :LANES]
    kcat_ref[:, LANES:] = kr_ref[0]
    vaug_ref[:, :LANES] = kv_ref[0, :, LANES:]
    ones_col = lax.broadcasted_iota(jnp.int32, (seq, LANES), 1) == 0
    vaug_ref[:, LANES:] = jnp.where(ones_col, 1.0, 0.0).astype(BF16)
    rq = lax.broadcasted_iota(jnp.int32, (tq, tq), 0) // CHUNK
    ck = lax.broadcasted_iota(jnp.int32, (tq, tq), 1) // CHUNK
    diag_ok = ck <= rq
    expo = MLA_SCALE * LOG2_E
    for i in range(seq // tq):
        lo = i * tq
        q = q_ref[0, lo:lo + tq, :]
        s_d = jnp.where(diag_ok, _dot_nt(q, kcat_ref[lo:lo + tq, :]), NEG_BIG)
        m = jnp.max(s_d, -1, keepdims=True)
        if i > 0:
            s_f = _dot_nt(q, kcat_ref[:lo, :])
            m = jnp.maximum(m, jnp.max(s_f, -1, keepdims=True))
        o = _dot(jnp.exp2((s_d - m) * expo).astype(BF16), vaug_ref[lo:lo + tq, :])
        if i > 0:
            o = o + _dot(jnp.exp2((s_f - m) * expo).astype(BF16), vaug_ref[:lo, :])
        o_ref[0, lo:lo + tq, :] = (o[:, :LANES] / o[:, LANES:LANES + 1]).astype(BF16)


def _attention(q, kv, kr, batch, seq):
    tq = _blk(seq, 256)
    return pl.pallas_call(
        functools.partial(_attn_kernel, seq=seq, tq=tq),
        grid=(batch, MLA_HEADS),
        in_specs=[
            pl.BlockSpec((1, seq, Q_HEAD_PAD), lambda b, h: (b, 0, h)),
            pl.BlockSpec((1, seq, QK_NOPE + V_DIM), lambda b, h: (b, 0, h)),
            pl.BlockSpec((1, seq, LANES), lambda b, h: (b, 0, 0)),
        ],
        out_specs=pl.BlockSpec((1, seq, V_DIM), lambda b, h: (b, 0, h)),
        out_shape=jax.ShapeDtypeStruct((batch, seq, MLA_HEADS * V_DIM), BF16),
        scratch_shapes=[pltpu.VMEM((seq, Q_HEAD_PAD), BF16), pltpu.VMEM((seq, V_DIM + LANES), BF16)],
        compiler_params=_params(("parallel", "parallel")),
        name="mla_attention",
    )(q, kv, kr)


def _xproj_kernel(x_ref, wa_ref, wg_ref, wm_ref, u_ref, mq_ref):
    x = x_ref[...]
    u_ref[...] = _dot(x, wa_ref[...]) * jax.nn.sigmoid(_dot(x, wg_ref[...]))
    mq_ref[...] = _dot(x, wm_ref[...]).astype(BF16)


def _xproj(l, x_bf, w_in_p, col_a, col_g, col_m, n):
    t, d = x_bf.shape
    bm, bn = _blk(t, 512), _blk(n, 512)
    wspec = lambda col: pl.BlockSpec((None, d, bn), lambda j, i: (l, 0, col // bn + j))
    ospec = pl.BlockSpec((bm, bn), lambda j, i: (i, j))
    return pl.pallas_call(
        _xproj_kernel,
        grid=(n // bn, t // bm),
        in_specs=[pl.BlockSpec((bm, d), lambda j, i: (i, 0)), wspec(col_a), wspec(col_g), wspec(col_m)],
        out_specs=[ospec, ospec],
        out_shape=[jax.ShapeDtypeStruct((t, n), F32), jax.ShapeDtypeStruct((t, n), BF16)],
        compiler_params=_params(("parallel", "parallel")),
        name="xproj_glu_memq",
    )(x_bf, w_in_p, w_in_p, w_in_p)


def _mm_kernel(a_ref, w_ref, o_ref):
    o_ref[...] = _dot(a_ref[...], w_ref[...]).astype(o_ref.dtype)


def _matmul_all_layers(a, w, out_dtype):
    m, k = a.shape
    depth, _, n = w.shape
    bm, bn = _blk(m, 512), _blk(n, 512)
    return pl.pallas_call(
        _mm_kernel,
        grid=(depth, n // bn, m // bm),
        in_specs=[pl.BlockSpec((bm, k), lambda l, j, i: (i, 0)),
                  pl.BlockSpec((None, k, bn), lambda l, j, i: (l, 0, j))],
        out_specs=pl.BlockSpec((None, bm, bn), lambda l, j, i: (l, i, j)),
        out_shape=jax.ShapeDtypeStruct((depth, m, n), out_dtype),
        compiler_params=_params(("parallel", "parallel", "parallel")),
        name="matmul_all_layers",
    )(a, w)


def _conv_kernel(cur_ref, prev_ref, w_ref, b_ref, g_ref, beta_ref, o_ref, win_ref, *, ts, rt):
    i = pl.program_id(1)
    win_ref[0:CONV_HALO, :] = jnp.where(i > 0, prev_ref[0, ts - CONV_HALO:ts, :], 0.0)
    win_ref[CONV_HALO:CONV_HALO + ts, :] = cur_ref[0]
    win_ref[CONV_HALO + ts:, :] = jnp.zeros((SUBLANES, win_ref.shape[1]), F32)
    first_tap = CONV_HALO - (CONV_WIDTH - 1)
    for r in range(ts // rt):
        r0 = r * rt
        acc = None
        for phase in range(SUBLANES):
            ys = None
            for k in range(CONV_WIDTH):
                off = first_tap + k
                if off % SUBLANES != phase:
                    continue
                base = r0 + off - phase
                term = win_ref[base:base + rt + SUBLANES, :] * w_ref[k:k + 1, :]
                ys = term if ys is None else ys + term
            part = ys[phase:phase + rt, :]
            acc = part if acc is None else acc + part
        v = _layer_norm_rows(acc + b_ref[...], g_ref[...], beta_ref[...])
        o_ref[0, r0:r0 + rt, :] = (v * jax.nn.sigmoid(v)).astype(BF16)


def _conv_module(l, u0, conv_w, conv_b, ln_g, ln_b, batch, seq):
    c = u0.shape[-1]
    ts = _blk(seq, 256)
    assert ts >= CONV_HALO
    rt = _blk(ts, 64)
    vec = lambda b, i: (l, 0, 0)
    return pl.pallas_call(
        functools.partial(_conv_kernel, ts=ts, rt=rt),
        grid=(batch, seq // ts),
        in_specs=[
            pl.BlockSpec((1, ts, c), lambda b, i: (b, i, 0)),
            pl.BlockSpec((1, ts, c), lambda b, i: (b, jnp.maximum(i - 1, 0), 0)),
            pl.BlockSpec((None, CONV_HALO, c), vec),
            pl.BlockSpec((None, 1, c), vec),
            pl.BlockSpec((None, 1, c), vec),
            pl.BlockSpec((None, 1, c), vec),
        ],
        out_specs=pl.BlockSpec((1, ts, c), lambda b, i: (b, i, 0)),
        out_shape=jax.ShapeDtypeStruct((batch, seq, c), BF16),
        scratch_shapes=[pltpu.VMEM((CONV_HALO + ts + SUBLANES, c), F32)],
        compiler_params=_params(("parallel", "arbitrary")),
        name="conv_module",
    )(u0, u0, conv_w, conv_b, ln_g, ln_b)


def _mem_attn_kernel(q_ref, kv_ref, o_ref):
    width = MEM_HEADS * MEM_HEAD_DIM
    for h in range(MEM_HEADS):
        lo = h * MEM_HEAD_DIM
        s = _dot_nt(q_ref[0, :, lo:lo + MEM_HEAD_DIM], kv_ref[0, :, lo:lo + MEM_HEAD_DIM]) * MEM_SCALE
        p = jnp.exp(s - jnp.max(s, -1, keepdims=True))
        l = jnp.sum(p, -1, keepdims=True)
        o = _dot(p.astype(BF16), kv_ref[0, :, width + lo:width + lo + MEM_HEAD_DIM])
        o_ref[0, :, lo:lo + MEM_HEAD_DIM] = (o / l).astype(BF16)


def _mem_attention(l, mq, kvm, batch, seq):
    width = MEM_HEADS * MEM_HEAD_DIM
    m = kvm.shape[1]
    ts = _blk(seq, 512)
    return pl.pallas_call(
        _mem_attn_kernel,
        grid=(batch, seq // ts),
        in_specs=[
            pl.BlockSpec((1, ts, width), lambda b, i: (b, i, 0)),
            pl.BlockSpec((1, m, 2 * width), lambda b, i: (l * batch + b, 0, 0)),
        ],
        out_specs=pl.BlockSpec((1, ts, width), lambda b, i: (b, i, 0)),
        out_shape=jax.ShapeDtypeStruct((batch, seq, width), BF16),
        compiler_params=_params(("parallel", "parallel")),
        name="mem_attention",
    )(mq, kvm)


def _merge_kernel(x_ref, o_ref, u_ref, m_ref, wg0_ref, wg1_ref, wg2_ref, w0_ref, w1_ref, w2_ref, z_ref):
    x = x_ref[...]
    z = jax.nn.sigmoid(_dot(x, wg0_ref[...])) * _dot(o_ref[...], w0_ref[...])
    z = z + jax.nn.sigmoid(_dot(x, wg1_ref[...])) * _dot(u_ref[...], w1_ref[...])
    z = z + jax.nn.sigmoid(_dot(x, wg2_ref[...])) * _dot(m_ref[...], w2_ref[...])
    z_ref[...] = z.astype(BF16)


def _merge(l, x_bf, o_mla, u_conv, o_mem, w_in_p, col_gate, w_o_mla, w_conv_out, w_o_mem):
    t, d = x_bf.shape
    bm, bn = _blk(t, 512), _blk(d, 256)
    nj = d // bn
    act = lambda a: pl.BlockSpec((bm, a.shape[1]), lambda j, i: (i, 0))
    gate = lambda b: pl.BlockSpec((None, d, bn), lambda j, i, b=b: (l, 0, col_gate // bn + b * nj + j))
    wbr = lambda w: pl.BlockSpec((None, w.shape[1], bn), lambda j, i: (l, 0, j))
    return pl.pallas_call(
        _merge_kernel,
        grid=(nj, t // bm),
        in_specs=[act(x_bf), act(o_mla), act(u_conv), act(o_mem), gate(0), gate(1), gate(2),
                  wbr(w_o_mla), wbr(w_conv_out), wbr(w_o_mem)],
        out_specs=pl.BlockSpec((bm, bn), lambda j, i: (i, j)),
        out_shape=jax.ShapeDtypeStruct((t, d), BF16),
        compiler_params=_params(("parallel", "parallel")),
        name="gated_merge",
    )(x_bf, o_mla, u_conv, o_mem, w_in_p, w_in_p, w_in_p, w_o_mla, w_conv_out, w_o_mem)


def _route_rows(logits, b_router):
    scores = jax.nn.sigmoid(logits[:N_EXPERTS])
    biased = scores + b_router[:N_EXPERTS]
    sc = [scores[e:e + 1] for e in range(N_EXPERTS)]
    bg = [biased[e:e + 1] for e in range(N_EXPERTS)]
    group_score = []
    for g in range(N_GROUPS):
        v = bg[g * EXPERTS_PER_GROUP:(g + 1) * EXPERTS_PER_GROUP]
        best = None
        for a in range(EXPERTS_PER_GROUP):
            for b in range(a + 1, EXPERTS_PER_GROUP):
                pair = v[a] + v[b]
                best = pair if best is None else jnp.maximum(best, pair)
        group_score.append(best)
    best, best_g = group_score[0], jnp.zeros_like(group_score[0], dtype=jnp.int32)
    for g in range(1, N_GROUPS):
        better = group_score[g] > best
        best = jnp.where(better, group_score[g], best)
        best_g = jnp.where(better, g, best_g)
    sel, picked = [], []
    for e in range(N_EXPERTS):
        g, j = divmod(e, EXPERTS_PER_GROUP)
        rank = jnp.zeros_like(best_g)
        for jj in range(EXPERTS_PER_GROUP):
            if jj == j:
                continue
            other = bg[g * EXPERTS_PER_GROUP + jj]
            ahead = (other > bg[e]) | ((other == bg[e]) & (jj < j))
            rank = rank + ahead.astype(jnp.int32)
        s = (best_g == g) & (rank < TOP_K)
        sel.append(s)
        picked.append(jnp.where(s, sc[e], 0.0))
    denom = picked[0]
    for e in range(1, N_EXPERTS):
        denom = denom + picked[e]
    return [p / denom for p in picked], sel


def _pack_bf16_pair(lo, hi):
    lo_bits = lax.bitcast_convert_type(lo.astype(F32), jnp.uint32) >> 16
    hi_bits = lax.bitcast_convert_type(hi.astype(F32), jnp.uint32) & jnp.uint32(0xFFFF0000)
    return hi_bits | lo_bits


def _unpack_bf16_pair(words):
    lo = lax.bitcast_convert_type(words << 16, F32).astype(BF16)
    hi = lax.bitcast_convert_type(words & jnp.uint32(0xFFFF0000), F32).astype(BF16)
    return lo, hi


def _outproj_ln_kernel(z_ref, w_ref, x_ref, g_ref, b_ref, wr_ref, br_ref,
                       xo_ref, xpk_ref, route_ref, acc_ref, *, alpha, nj, bn):
    j = pl.program_id(1)
    acc_ref[j] = alpha * x_ref[...] + _dot(z_ref[...], w_ref[...])

    @pl.when(j == nj - 1)
    def _():
        d = nj * bn
        tot = jnp.sum(acc_ref[0], -1, keepdims=True)
        for jj in range(1, nj):
            tot = tot + jnp.sum(acc_ref[jj], -1, keepdims=True)
        mu = tot / d
        sq = None
        for jj in range(nj):
            dv = acc_ref[jj] - mu
            part = jnp.sum(dv * dv, -1, keepdims=True)
            sq = part if sq is None else sq + part
        rstd = lax.rsqrt(sq / d + LN_EPS)
        logits = None
        half = nj // 2
        for jj in range(half):
            pair = []
            for c in (jj, jj + half):
                lo = c * bn
                v = (acc_ref[c] - mu) * rstd * g_ref[:, lo:lo + bn] + b_ref[:, lo:lo + bn]
                xo_ref[:, lo:lo + bn] = v
                vb = v.astype(BF16)
                part = _dot_nt(wr_ref[:, lo:lo + bn], vb)
                logits = part if logits is None else logits + part
                pair.append(vb)
            xpk_ref[:, jj * bn:(jj + 1) * bn] = _pack_bf16_pair(pair[0], pair[1])
        cw, sel = _route_rows(logits, br_ref[...])
        for e in range(N_EXPERTS):
            route_ref[e:e + 1, :] = cw[e]
            route_ref[N_EXPERTS + e:N_EXPERTS + e + 1, :] = sel[e].astype(F32)


def _outproj_ln_route(l, z, w_out, x_res, ln_g, ln_b, w_router_t, b_router_col, alpha):
    t, d = z.shape
    bm, bn = _blk(t, 512), _blk(d // 2, 512)
    nj = d // bn
    fixed = lambda i, j: (0, 0)
    layer = lambda i, j: (l, 0, 0)
    return pl.pallas_call(
        functools.partial(_outproj_ln_kernel, alpha=alpha, nj=nj, bn=bn),
        grid=(t // bm, nj),
        in_specs=[
            pl.BlockSpec((bm, d), lambda i, j: (i, 0)),
            pl.BlockSpec((None, d, bn), lambda i, j: (l, 0, j)),
            pl.BlockSpec((bm, bn), lambda i, j: (i, j)),
            _resident((None, 1, d), layer),
            _resident((None, 1, d), layer),
            _resident((LANES, d), fixed),
            _resident((LANES, 1), fixed),
        ],
        out_specs=[
            pl.BlockSpec((bm, d), lambda i, j: (i, 0)),
            pl.BlockSpec((bm, d // 2), lambda i, j: (i, 0)),
            pl.BlockSpec((ROUTE_ROWS, bm), lambda i, j: (0, i)),
        ],
        out_shape=[
            jax.ShapeDtypeStruct((t, d), F32),
            jax.ShapeDtypeStruct((t, d // 2), jnp.uint32),
            jax.ShapeDtypeStruct((ROUTE_ROWS, t), F32),
        ],
        scratch_shapes=[pltpu.VMEM((nj, bm, bn), F32)],
        compiler_params=_params(("parallel", "arbitrary"), VMEM_LIMIT_OUTPROJ),
        name="outproj_ln_route",
    )(z, w_out, x_res, ln_g, ln_b, w_router_t, b_router_col)


def _row_copy(src_hbm, row, dst_ref, r, sem):
    return pltpu.make_async_copy(src_hbm.at[pl.ds(row, 1), :], dst_ref.at[pl.ds(r, 1), :], sem)


def _block_copy(src_hbm, dst_ref, sem):
    return pltpu.make_async_copy(src_hbm.at[pl.ds(0, dst_ref.shape[0]), :], dst_ref, sem)


FFN_BUFFERS = 3


def _ffn_kernel(be_ref, nused_ref, tok_a, tok_b, tok_c, x_hbm, wg_ref, wu_ref, wd_ref, y_ref,
                xbuf0, xbuf1, xbuf2, sems, *, rows):
    i = pl.program_id(0)
    n_used = nused_ref[0]
    bufs = (xbuf0, xbuf1, xbuf2)
    half = wg_ref.shape[0] // 2

    def gather(tok_ref, s):
        for r in range(rows):
            _row_copy(x_hbm, tok_ref[0, 0, r], bufs[s], r, sems.at[s]).start()

    @pl.when(i == 0)
    def _():
        gather(tok_a, 0)
        gather(tok_b, 1)

    @pl.when(i >= n_used)
    def _():
        y_ref[...] = jnp.zeros(y_ref.shape, y_ref.dtype)

    for s in range(FFN_BUFFERS):
        mine = i % FFN_BUFFERS == s

        @pl.when(mine & (i < n_used + 2))
        def _():
            _block_copy(x_hbm, bufs[s], sems.at[s]).wait()

        @pl.when(mine & (i < n_used))
        def _():
            gather(tok_c, (s + 2) % FFN_BUFFERS)
            x_lo, x_hi = _unpack_bf16_pair(bufs[s][...])
            g = _dot(x_lo, wg_ref[:half, :]) + _dot(x_hi, wg_ref[half:, :])
            u = _dot(x_lo, wu_ref[:half, :]) + _dot(x_hi, wu_ref[half:, :])
            h = (g * jax.nn.sigmoid(g)) * u
            y_ref[...] = _dot(h.astype(BF16), wd_ref[...])


def _expert_ffn(l, x_packed, slot_tok, block_e, n_used, w_gate, w_up, w_down, rows):
    p = slot_tok.shape[0]
    dh = x_packed.shape[1]
    d = 2 * dh
    de = w_gate.shape[3]
    nb = p // rows
    tok = slot_tok.reshape(nb, 1, rows)
    wsel = lambda i, be, nu: (l, be[jnp.minimum(i, nu[0] - 1)], 0, 0)
    tok_block = lambda f: pl.BlockSpec((1, 1, rows), lambda i, be, nu: (f(i), 0, 0), memory_space=pltpu.SMEM)
    return pl.pallas_call(
        functools.partial(_ffn_kernel, rows=rows),
        grid_spec=pltpu.PrefetchScalarGridSpec(
            num_scalar_prefetch=2,
            grid=(nb,),
            in_specs=[
                tok_block(lambda i: 0),
                tok_block(lambda i: 1),
                tok_block(lambda i: jnp.minimum(i + 2, nb - 1)),
                pl.BlockSpec(memory_space=pl.ANY),
                pl.BlockSpec((None, None, d, de), wsel),
                pl.BlockSpec((None, None, d, de), wsel),
                pl.BlockSpec((None, None, de, d), wsel),
            ],
            out_specs=pl.BlockSpec((rows, d), lambda i, be, nu: (i, 0)),
            scratch_shapes=[pltpu.VMEM((rows, dh), jnp.uint32)] * FFN_BUFFERS
            + [pltpu.SemaphoreType.DMA((FFN_BUFFERS,))],
        ),
        out_shape=jax.ShapeDtypeStruct((p, d), F32),
        compiler_params=_params(("arbitrary",)),
        name="moe_expert_ffn",
    )(block_e, n_used, tok, tok, tok, x_packed, w_gate, w_up, w_down)


def _combine_kernel(pos0_ref, pos1_ref, ys_hbm, x_ref, wts_ref, g_ref, b_ref, xo_ref, xbf_ref,
                    a0, b0, a1, b1, sems, *, alpha, rows, nb):
    i = pl.program_id(0)
    bufs = ((a0, b0), (a1, b1))

    def gather(pos_ref, s):
        for r in range(rows):
            _row_copy(ys_hbm, pos_ref[0, 0, r], bufs[s][0], r, sems.at[s]).start(priority=0)
            _row_copy(ys_hbm, pos_ref[0, 1, r], bufs[s][1], r, sems.at[s]).start(priority=1)

    def drain(s):
        _block_copy(ys_hbm, bufs[s][0], sems.at[s]).wait()
        _block_copy(ys_hbm, bufs[s][1], sems.at[s]).wait()

    @pl.when(i == 0)
    def _():
        gather(pos0_ref, 0)

    for s in range(2):
        @pl.when(i % 2 == s)
        def _():
            drain(s)
            gather(pos1_ref, 1 - s)
            y = wts_ref[:, 0:1] * bufs[s][0][...] + wts_ref[:, 1:2] * bufs[s][1][...]
            v = _layer_norm_rows(alpha * x_ref[...] + y, g_ref[...], b_ref[...])
            xo_ref[...] = v
            xbf_ref[...] = v.astype(BF16)

        @pl.when((i % 2 == s) & (i == nb - 1))
        def _():
            drain(1 - s)


def _combine_ln(l, ys, pos, wts, x_res, ln_g, ln_b, alpha):
    t, d = x_res.shape
    rows = _blk(t, 128)
    nb = t // rows
    layer = lambda i: (l, 0, 0)
    pos_blocks = pos.reshape(nb, rows, 2).transpose(0, 2, 1)
    return pl.pallas_call(
        functools.partial(_combine_kernel, alpha=alpha, rows=rows, nb=nb),
        grid=(nb,),
        in_specs=[
            pl.BlockSpec((1, 2, rows), lambda i: (0, 0, 0), memory_space=pltpu.SMEM),
            pl.BlockSpec((1, 2, rows), lambda i: (jnp.minimum(i + 1, nb - 1), 0, 0), memory_space=pltpu.SMEM),
            pl.BlockSpec(memory_space=pl.ANY),
            pl.BlockSpec((rows, d), lambda i: (i, 0)),
            pl.BlockSpec((rows, LANES), lambda i: (i, 0)),
            _resident((None, 1, d), layer),
            _resident((None, 1, d), layer),
        ],
        out_specs=[pl.BlockSpec((rows, d), lambda i: (i, 0)), pl.BlockSpec((rows, d), lambda i: (i, 0))],
        out_shape=[jax.ShapeDtypeStruct((t, d), F32), jax.ShapeDtypeStruct((t, d), BF16)],
        scratch_shapes=[pltpu.VMEM((rows, d), F32)] * 4 + [pltpu.SemaphoreType.DMA((2,))],
        compiler_params=_params(("arbitrary",)),
        name="moe_combine_ln",
    )(pos_blocks, pos_blocks, ys, x_res, wts, ln_g, ln_b)


def _dispatch_plan(route_t, rows):
    t = route_t.shape[1]
    cw = route_t[:N_EXPERTS].T
    sel = route_t[N_EXPERTS:].T > 0.5
    sel_i = sel.astype(jnp.int32)
    csum = jnp.cumsum(sel_i, axis=0)
    counts = csum[-1]
    padded = (counts + rows - 1) // rows * rows
    seg_end = jnp.cumsum(padded)
    dest = (seg_end - padded)[None, :] + (csum - sel_i)
    nb = t * TOP_K // rows + N_EXPERTS + FFN_BUFFERS - 1
    n_slots = nb * rows
    pos0 = jnp.min(jnp.where(sel, dest, n_slots), axis=1)
    pos1 = jnp.max(jnp.where(sel, dest, -1), axis=1)
    w0 = jnp.sum(jnp.where(sel & (dest == pos0[:, None]), cw, 0.0), axis=1)
    w1 = jnp.sum(jnp.where(sel & (dest == pos1[:, None]), cw, 0.0), axis=1)
    tok = jnp.arange(t, dtype=jnp.int32)
    slot_tok = jnp.zeros((n_slots,), jnp.int32).at[jnp.concatenate([pos0, pos1])].set(
        jnp.concatenate([tok, tok]), unique_indices=True)
    block_start = jnp.arange(nb, dtype=jnp.int32) * rows
    block_e = jnp.minimum(jnp.sum((seg_end[None, :] <= block_start[:, None]).astype(jnp.int32), axis=1),
                          N_EXPERTS - 1).astype(jnp.int32)
    n_used = (seg_end[-1] // rows).astype(jnp.int32).reshape(1)
    pos = jnp.stack([pos0, pos1], axis=1).astype(jnp.int32)
    wts = jnp.pad(jnp.stack([w0, w1], axis=1), ((0, 0), (0, LANES - TOP_K)))
    return slot_tok, block_e, n_used, pos, wts


def _rope_tables(positions):
    inv = 1.0 / (ROPE_THETA ** (jnp.arange(0, QK_ROPE, 2, dtype=F32) / QK_ROPE))
    ang = positions.astype(F32).reshape(-1, 1) * inv
    cos, sin = jnp.cos(ang), jnp.sin(ang)
    zero = jnp.zeros_like(cos)
    pad = jnp.zeros((cos.shape[0], LANES - QK_ROPE), F32)
    c_tab = jnp.concatenate([cos, cos, pad], 1)
    s1_tab = jnp.concatenate([zero, sin, pad], 1)
    s2_tab = jnp.concatenate([-sin, zero, pad], 1)
    return c_tab, s1_tab, s2_tab


def kernel(x, mem, positions, w_in, q_norm_g, kv_norm_g, w_uq, w_ukv, w_o_mla, conv_w, conv_b, conv_ln_g, conv_ln_b, w_conv_out, w_mem_kv, w_o_mem, w_out, ln1_g, ln1_b, w_router, b_router, w_gate, w_up, w_down, ln2_g, ln2_b):
    batch, seq, d = x.shape
    depth = w_in.shape[0]
    t = batch * seq
    q_lora, kv_lora, conv_dim = q_norm_g.shape[1], kv_norm_g.shape[1], conv_w.shape[2]
    mem_w = MEM_HEADS * MEM_HEAD_DIM
    assert conv_dim == mem_w
    alpha = (2 * depth) ** 0.25
    moe_rows = 256

    c_tab, s1_tab, s2_tab = _rope_tables(positions)
    mem_bf = mem.reshape(-1, d).astype(BF16)
    w_router_t = jnp.pad(w_router.T, ((0, LANES - N_EXPERTS), (0, 0))).astype(BF16)
    b_router_col = jnp.pad(b_router.astype(F32), (0, LANES - N_EXPERTS)).reshape(LANES, 1)

    o_conv = q_lora + kv_lora + QK_ROPE
    wa_cols = -(-(o_conv + LANES - QK_ROPE) // W_IN_ALIGN) * W_IN_ALIGN
    w_all = _cast_transposed(jnp.swapaxes(w_in, 1, 2), o_conv, wa_cols - o_conv)
    col_a, col_g, col_m = wa_cols, wa_cols + conv_dim, wa_cols + 2 * conv_dim
    col_gate = col_m + mem_w

    wuq = jnp.pad(w_uq.reshape(depth, q_lora, MLA_HEADS, QK_NOPE + QK_ROPE),
                  ((0, 0), (0, 0), (0, 0), (0, Q_HEAD_PAD - QK_NOPE - QK_ROPE))
                  ).reshape(depth, q_lora, -1).astype(BF16)
    wukv, w_o_mla_b, w_conv_out_b = w_ukv.astype(BF16), w_o_mla.astype(BF16), w_conv_out.astype(BF16)
    w_mem_kv_b, w_o_mem_b, w_out_b = w_mem_kv.astype(BF16), w_o_mem.astype(BF16), w_out.astype(BF16)
    w_gate_b, w_up_b, w_down_b = w_gate.astype(BF16), w_up.astype(BF16), w_down.astype(BF16)
    conv_w_p = jnp.pad(conv_w, ((0, 0), (0, CONV_HALO - CONV_WIDTH), (0, 0)))
    vec = lambda a: a[:, None, :]
    qg, kvg, conv_b3, conv_g3, conv_beta3 = vec(q_norm_g), vec(kv_norm_g), vec(conv_b), vec(conv_ln_g), vec(conv_ln_b)
    ln1_g3, ln1_b3, ln2_g3, ln2_b3 = vec(ln1_g), vec(ln1_b), vec(ln2_g), vec(ln2_b)

    kvm_all = _matmul_all_layers(mem_bf, w_mem_kv_b, BF16).reshape(depth * batch, -1, 2 * mem_w)

    xf = x.reshape(t, d)
    xb = xf.astype(BF16)
    for l in range(depth):
        q, kv, kr = _mla_proj(l, xb, w_all, o_conv + LANES - QK_ROPE, wuq, wukv, qg, kvg, c_tab, s1_tab, s2_tab)
        o_mla = _attention(q.reshape(batch, seq, -1), kv.reshape(batch, seq, -1),
                           kr.reshape(batch, seq, LANES), batch, seq).reshape(t, -1)

        u0, mq = _xproj(l, xb, w_all, col_a, col_g, col_m, conv_dim)
        u_conv = _conv_module(l, u0.reshape(batch, seq, conv_dim), conv_w_p, conv_b3, conv_g3, conv_beta3,
                              batch, seq).reshape(t, -1)

        o_mem = _mem_attention(l, mq.reshape(batch, seq, mem_w), kvm_all, batch, seq).reshape(t, -1)

        z = _merge(l, xb, o_mla, u_conv, o_mem, w_all, col_gate, w_o_mla_b, w_conv_out_b, w_o_mem_b)
        x1, x1_packed, route_t = _outproj_ln_route(l, z, w_out_b, xf, ln1_g3, ln1_b3, w_router_t, b_router_col,
                                                   alpha)

        slot_tok, block_e, n_used, pos, wts = _dispatch_plan(route_t, moe_rows)
        ys = _expert_ffn(l, x1_packed, slot_tok, block_e, n_used, w_gate_b, w_up_b, w_down_b, moe_rows)
        xf, xb = _combine_ln(l, ys, pos, wts, x1, ln2_g3, ln2_b3, alpha)
    return xf.reshape(batch, seq, d)
```
